```python
import jax, jax.numpy as jnp
from jax import lax
import numpy as np

D_MODEL = 1024
BATCH = 2
SEQ = 8192
DEPTH = 1

D_MIX = D_MODEL
ATT_HEAD_DIM = 64
ATT_HEADS = (D_MIX // 2) // ATT_HEAD_DIM
ATT_WIDTH = ATT_HEADS * ATT_HEAD_DIM
IDX_HEADS = 8
IDX_HEAD_DIM = 64
TOPK_MAX = 256
Q_BLOCK = 128
DN_HEAD_DIM = 128
DN_HEADS = (D_MIX - ATT_WIDTH) // DN_HEAD_DIM
DN_WIDTH = DN_HEADS * DN_HEAD_DIM
CONV_KERNEL = 4
CHUNK = 64
EPS = 1e-6
NEG = -1e30

SPLITS = (ATT_WIDTH, ATT_WIDTH, ATT_WIDTH, ATT_WIDTH,
          IDX_HEADS * IDX_HEAD_DIM, IDX_HEAD_DIM, IDX_HEADS,
          DN_WIDTH, DN_WIDTH, DN_WIDTH, DN_WIDTH,
          DN_HEADS, DN_HEADS)
D_IN = 4 * ATT_WIDTH + IDX_HEADS * IDX_HEAD_DIM + IDX_HEAD_DIM + IDX_HEADS + 4 * DN_WIDTH + 2 * DN_HEADS

kernel_name = "hybrid_dsa_gated_deltanet_parallel_heads"


def rms_norm(x, w):
    xf = x.astype(jnp.float32)
    y = xf * lax.rsqrt(jnp.mean(xf * xf, axis=-1, keepdims=True) + EPS)
    return (y * w.astype(jnp.float32)).astype(x.dtype)


def layer_norm_f32(x, w, b):
    xf = x.astype(jnp.float32)
    mu = jnp.mean(xf, axis=-1, keepdims=True)
    var = jnp.mean(jnp.square(xf - mu), axis=-1, keepdims=True)
    return (xf - mu) * lax.rsqrt(var + EPS) * w.astype(jnp.float32) + b.astype(jnp.float32)


def l2_norm_f32(x):
    return x * lax.rsqrt(jnp.sum(x * x, axis=-1, keepdims=True) + EPS)


def causal_depthwise_conv(x, w):
    width = w.shape[0]
    return lax.conv_general_dilated(
        x, w[:, None, :].astype(x.dtype), window_strides=(1,), padding=[(width - 1, 0)],
        dimension_numbers=("NWC", "WIO", "NWC"), feature_group_count=x.shape[-1])


def dsa_sparse_attention(q, k, v, q_idx, k_idx, w_idx):
    B, L, H, D = q.shape
    n_blk = L // Q_BLOCK
    topk = min(TOPK_MAX, L // 4)
    key_pos = jnp.arange(L)

    def blocks(t):
        return t.reshape(B, n_blk, Q_BLOCK, *t.shape[2:]).swapaxes(0, 1)

    def one_block(args):
        qb, qib, wb, blk = args
        q_pos = blk * Q_BLOCK + jnp.arange(Q_BLOCK)
        s = jnp.einsum("bqhd,bsd->bqhs", qib, k_idx)
        score = jnp.einsum("bqhs,bqh->bqs", jax.nn.relu(s), wb)
        causal = key_pos[None, :] <= q_pos[:, None]
        score = jnp.where(causal[None], score, NEG)
        _, idx = lax.top_k(score, topk)
        k_sel = jax.vmap(lambda kb, ib: kb[ib])(k, idx)
        v_sel = jax.vmap(lambda vb, ib: vb[ib])(v, idx)
        logits = jnp.einsum("bqhd,bqkhd->bhqk", qb.astype(jnp.float32),
                            k_sel.astype(jnp.float32)) * (D ** -0.5)
        valid = idx <= q_pos[None, :, None]
        logits = jnp.where(valid[:, None], logits, NEG)
        p = jax.nn.softmax(logits, axis=-1)
        o = jnp.einsum("bhqk,bqkhd->bqhd", p, v_sel.astype(jnp.float32))
        return o.astype(q.dtype)

    out = lax.map(one_block, (blocks(q), blocks(q_idx), blocks(w_idx), jnp.arange(n_blk)))
    return out.swapaxes(0, 1).reshape(B, L, H, D)


def chunk_gated_delta_rule(q, k, v, beta, g):
    B, L, H, Dk = q.shape
    Dv = v.shape[-1]
    N = L // CHUNK
    q = q * (Dk ** -0.5)

    def chunks(t):
        t = t.reshape(B, N, CHUNK, H, *t.shape[3:])
        return jnp.moveaxis(t, 3, 1)

    qc, kc, vc = chunks(q), chunks(k), chunks(v)
    bc, gc = chunks(beta), chunks(g)
    g_cum = jnp.cumsum(gc, axis=-1)
    lower = jnp.tril(jnp.ones((CHUNK, CHUNK), dtype=bool))
    strict = jnp.tril(jnp.ones((CHUNK, CHUNK), dtype=bool), -1)
    diff = g_cum[..., :, None] - g_cum[..., None, :]
    decay = jnp.exp(jnp.where(lower, diff, NEG))
    k_beta = kc * bc[..., None]
    v_beta = vc * bc[..., None]
    a_strict = jnp.where(strict, jnp.einsum("bhncd,bhnsd->bhncs", k_beta, kc) * decay, 0.0)
    m = a_strict + jnp.eye(CHUNK, dtype=a_strict.dtype)
    rhs = jnp.concatenate([v_beta, k_beta * jnp.exp(g_cum)[..., None]], axis=-1)
    sol = lax.linalg.triangular_solve(m, rhs, left_side=True, lower=True, unit_diagonal=True)
    u, w = sol[..., :Dv], sol[..., Dv:]
    intra = jnp.where(lower, jnp.einsum("bhncd,bhnsd->bhncs", qc, kc) * decay, 0.0)

    def to_scan(t):
        return jnp.moveaxis(t, 2, 0)

    def step(state, xs):
        q_i, k_i, u_i, w_i, a_i, g_i = xs
        v_new = u_i - jnp.einsum("bhcd,bhde->bhce", w_i, state)
        o_i = (jnp.einsum("bhcd,bhde->bhce", q_i * jnp.exp(g_i)[..., None], state)
               + jnp.einsum("bhcs,bhse->bhce", a_i, v_new))
        g_last = g_i[..., -1]
        state = (state * jnp.exp(g_last)[..., None, None]
                 + jnp.einsum("bhcd,bhce->bhde", k_i * jnp.exp(g_last[..., None] - g_i)[..., None], v_new))
        return state, o_i

    s0 = jnp.zeros((B, H, Dk, Dv), dtype=jnp.float32)
    _, o = lax.scan(step, s0, (to_scan(qc), to_scan(kc), to_scan(u), to_scan(w), to_scan(intra), to_scan(g_cum)))
    return jnp.transpose(o, (1, 0, 3, 2, 4)).reshape(B, L, H, Dv)


def setup_inputs(seed: int = 0) -> dict:
    key = jax.random.key(seed)
    ks = jax.random.split(key, 12)
    f32 = jnp.float32
    x = jax.random.normal(ks[0], (BATCH, SEQ, D_MODEL), f32)
    ln_w = 1.0 + 0.02 * jax.random.normal(ks[1], (DEPTH, D_MODEL), f32)
    w_in = jax.random.normal(ks[2], (DEPTH, D_MODEL, D_IN), f32) * D_MODEL ** -0.5
    attn_q_norm_w = 1.0 + 0.02 * jax.random.normal(ks[3], (DEPTH, ATT_HEAD_DIM), f32)
    attn_k_norm_w = 1.0 + 0.02 * jax.random.normal(ks[4], (DEPTH, ATT_HEAD_DIM), f32)
    idx_k_norm_w = 1.0 + 0.02 * jax.random.normal(ks[5], (DEPTH, IDX_HEAD_DIM), f32)
    idx_k_norm_b = 0.02 * jax.random.normal(ks[6], (DEPTH, IDX_HEAD_DIM), f32)
    dn_conv_w = jax.random.normal(ks[7], (DEPTH, CONV_KERNEL, 3 * DN_WIDTH), f32) * CONV_KERNEL ** -0.5
    dn_A_log = jnp.log(jax.random.uniform(ks[8], (DEPTH, DN_HEADS), f32, 1.0, 16.0))
    dt = jnp.exp(jax.random.uniform(ks[9], (DEPTH, DN_HEADS), f32, np.log(1e-3), np.log(1e-1)))
    dn_dt_bias = dt + jnp.log(-jnp.expm1(-dt))
    dn_norm_w = 1.0 + 0.02 * jax.random.normal(ks[10], (DEPTH, DN_HEAD_DIM), f32)
    w_out = jax.random.normal(ks[11], (DEPTH, D_MIX, D_MODEL), f32) * D_MIX ** -0.5
    return {"x": x, "ln_w": ln_w, "w_in": w_in, "attn_q_norm_w": attn_q_norm_w,
            "attn_k_norm_w": attn_k_norm_w, "idx_k_norm_w": idx_k_norm_w, "idx_k_norm_b": idx_k_norm_b,
            "dn_conv_w": dn_conv_w, "dn_A_log": dn_A_log, "dn_dt_bias": dn_dt_bias,
            "dn_norm_w": dn_norm_w, "w_out": w_out}


def reference(x, ln_w, w_in, attn_q_norm_w, attn_k_norm_w, idx_k_norm_w, idx_k_norm_b,
              dn_conv_w, dn_A_log, dn_dt_bias, dn_norm_w, w_out):
    B, L, _ = x.shape
    f32 = jnp.float32
    split_points = np.cumsum(np.array(SPLITS))[:-1].tolist()
    h = x
    for layer in range(DEPTH):
        hn = rms_norm(h, ln_w[layer])
        proj = hn @ w_in[layer]
        (aq, ak, av, ag, iq, ik, iw, dq, dk, dv, dz, db, da) = jnp.split(proj, split_points, axis=-1)

        q_a = rms_norm(aq.reshape(B, L, ATT_HEADS, ATT_HEAD_DIM), attn_q_norm_w[layer])
        k_a = rms_norm(ak.reshape(B, L, ATT_HEADS, ATT_HEAD_DIM), attn_k_norm_w[layer])
        v_a = av.reshape(B, L, ATT_HEADS, ATT_HEAD_DIM)
        q_i = iq.reshape(B, L, IDX_HEADS, IDX_HEAD_DIM).astype(f32)
        k_i = layer_norm_f32(ik, idx_k_norm_w[layer], idx_k_norm_b[layer])
        w_i = iw.astype(f32) * (IDX_HEADS ** -0.5) * (IDX_HEAD_DIM ** -0.5)
        o_a = dsa_sparse_attention(q_a, k_a, v_a, q_i, k_i, w_i).reshape(B, L, ATT_WIDTH)
        o_a = o_a * jax.nn.silu(ag)

        qkv = jax.nn.silu(causal_depthwise_conv(jnp.concatenate([dq, dk, dv], axis=-1), dn_conv_w[layer]))
        cq, ck, cv = jnp.split(qkv, [DN_WIDTH, 2 * DN_WIDTH], axis=-1)
        q_d = l2_norm_f32(cq.reshape(B, L, DN_HEADS, DN_HEAD_DIM).astype(f32))
        k_d = l2_norm_f32(ck.reshape(B, L, DN_HEADS, DN_HEAD_DIM).astype(f32))
        v_d = cv.reshape(B, L, DN_HEADS, DN_HEAD_DIM).astype(f32)
        beta = jax.nn.sigmoid(db.astype(f32))
        g = -jnp.exp(dn_A_log[layer].astype(f32)) * jax.nn.softplus(da.astype(f32) + dn_dt_bias[layer].astype(f32))
        o_d = chunk_gated_delta_rule(q_d, k_d, v_d, beta, g)
        o_d = rms_norm(o_d, dn_norm_w[layer]) * jax.nn.silu(dz.reshape(B, L, DN_HEADS, DN_HEAD_DIM).astype(f32))
        o_d = o_d.reshape(B, L, DN_WIDTH).astype(h.dtype)

        mix = jnp.concatenate([o_a.astype(h.dtype), o_d], axis=-1)
        h = h + mix @ w_out[layer]
    return h
```

```python
import functools

import jax
import jax.numpy as jnp
from jax import lax
from jax.experimental import pallas as pl
from jax.experimental.pallas import tpu as pltpu

F32 = jnp.float32
BF16 = jnp.bfloat16

D_MODEL = 1024
ATT_HEADS = 8
ATT_HEAD_DIM = 64
ATT_WIDTH = ATT_HEADS * ATT_HEAD_DIM
IDX_HEADS = 8
IDX_HEAD_DIM = 64
TOPK_MAX = 256
DN_HEADS = 4
DN_HEAD_DIM = 128
DN_WIDTH = DN_HEADS * DN_HEAD_DIM
CONV_KERNEL = 4
CHUNK = 64
EPS = 1e-6
NEG = -1e30
LANES = 128
LOWEST = -3.0e38

COL_AQ, COL_AK, COL_AV, COL_AG = 0, 512, 1024, 1536
COL_IQ = 2048
COL_IKW = 2560
COL_DQ, COL_DK, COL_DV, COL_DZ = 2688, 3200, 3712, 4224
COL_DBA = 4736
D_PAD = 4864
N_SPLIT = 2

VMEM_LIMIT = 60 * 1024 * 1024


def _cparams(sem):
    return pltpu.CompilerParams(dimension_semantics=sem, vmem_limit_bytes=VMEM_LIMIT)


def _proj_kernel(x_ref, lnw_ref, w_ref, o_ref):
    xf = x_ref[...]
    ms = jnp.mean(xf * xf, axis=-1, keepdims=True)
    hn = xf * lax.rsqrt(ms + EPS) * lnw_ref[...]
    o_ref[...] = jnp.dot(hn.astype(BF16), w_ref[...], preferred_element_type=F32)


def _proj_call(x2, ln_w, w_pad, tm=512):
    n = x2.shape[0]
    tn = D_PAD // N_SPLIT
    return pl.pallas_call(
        _proj_kernel,
        grid=(N_SPLIT, n // tm),
        in_specs=[pl.BlockSpec((tm, D_MODEL), lambda j, i: (i, 0)),
                  pl.BlockSpec((1, D_MODEL), lambda j, i: (0, 0)),
                  pl.BlockSpec((D_MODEL, tn), lambda j, i: (0, j))],
        out_specs=pl.BlockSpec((tm, tn), lambda j, i: (i, j)),
        out_shape=jax.ShapeDtypeStruct((n, D_PAD), F32),
        compiler_params=_cparams(("arbitrary", "arbitrary")),
        name="proj",
    )(x2, ln_w, w_pad)


def _group_sumsq(x, g):
    sq = x * x
    hi = sq.astype(BF16)
    lo = (sq - hi.astype(F32)).astype(BF16)
    return (jnp.dot(hi, g, preferred_element_type=F32) + jnp.dot(lo, g, preferred_element_type=F32))


def _attn_prep_kernel(aq_ref, ak_ref, av_ref, g_ref, wq_ref, wk_ref, q_ref, kt_ref, v_ref):
    g = g_ref[...]
    aq = aq_ref[...]
    ak = ak_ref[...]
    inv_d = 1.0 / ATT_HEAD_DIM
    qn = aq * lax.rsqrt(_group_sumsq(aq, g) * inv_d + EPS) * wq_ref[...]
    kn = ak * lax.rsqrt(_group_sumsq(ak, g) * inv_d + EPS) * wk_ref[...]
    q_ref[0] = (qn * (ATT_HEAD_DIM ** -0.5)).astype(BF16)
    kt_ref[0] = kn.T.astype(BF16)
    v_ref[0] = av_ref[...].astype(BF16)


def _attn_prep_call(proj, gmat, wq_t, wk_t, b, l, tr=512):
    nr = l // tr
    wblk = ATT_WIDTH
    return pl.pallas_call(
        _attn_prep_kernel,
        grid=(b, nr),
        in_specs=[pl.BlockSpec((tr, wblk), lambda bi, i: (bi * nr + i, COL_AQ // wblk)),
                  pl.BlockSpec((tr, wblk), lambda bi, i: (bi * nr + i, COL_AK // wblk)),
                  pl.BlockSpec((tr, wblk), lambda bi, i: (bi * nr + i, COL_AV // wblk)),
                  pl.BlockSpec((wblk, wblk), lambda bi, i: (0, 0)),
                  pl.BlockSpec((1, wblk), lambda bi, i: (0, 0)),
                  pl.BlockSpec((1, wblk), lambda bi, i: (0, 0))],
        out_specs=[pl.BlockSpec((1, tr, wblk), lambda bi, i: (bi, i, 0)),
                   pl.BlockSpec((1, wblk, tr), lambda bi, i: (bi, 0, i)),
                   pl.BlockSpec((1, tr, wblk), lambda bi, i: (bi, i, 0))],
        out_shape=[jax.ShapeDtypeStruct((b, l, wblk), BF16),
                   jax.ShapeDtypeStruct((b, wblk, l), BF16),
                   jax.ShapeDtypeStruct((b, l, wblk), BF16)],
        compiler_params=_cparams(("arbitrary", "arbitrary")),
        name="attn_prep",
    )(proj, proj, proj, gmat, wq_t, wk_t)


def _hi_lo(x):
    hi = x.astype(BF16).astype(F32)
    return hi, x - hi


def _idx_prep_kernel(iq_ref, ikw_ref, lnw_ref, lnb_ref, iqs_ref, kit_ref, wi_ref):
    tr = iq_ref.shape[0]
    lane = lax.broadcasted_iota(jnp.int32, (tr, LANES), 1)
    low = lane < IDX_HEAD_DIM
    zeros = jnp.zeros((tr, LANES), F32)

    for j in range(IDX_HEADS // 2):
        d = iq_ref[:, j * LANES:(j + 1) * LANES]
        r = pltpu.roll(d, IDX_HEAD_DIM, 1)
        for half, dup in enumerate((jnp.where(low, d, r), jnp.where(low, r, d))):
            hi, lo = _hi_lo(dup)
            h = 2 * j + half
            iqs_ref[0, h, :, 0:LANES] = jnp.where(low, hi, lo).astype(BF16)
            iqs_ref[0, h, :, LANES:2 * LANES] = jnp.where(low, hi, zeros).astype(BF16)

    ikw = ikw_ref[...]
    inv_d = 1.0 / IDX_HEAD_DIM
    mu = jnp.sum(jnp.where(low, ikw, 0.0), axis=-1, keepdims=True) * inv_d
    cen = jnp.where(low, ikw - mu, 0.0)
    var = jnp.sum(cen * cen, axis=-1, keepdims=True) * inv_d
    kn = jnp.where(low, cen * lax.rsqrt(var + EPS) * lnw_ref[...] + lnb_ref[...], 0.0)
    hi, lo = _hi_lo(kn)
    hh = hi + pltpu.roll(hi, IDX_HEAD_DIM, 1)
    kit_ref[0, 0:LANES, :] = hh.T.astype(BF16)
    kit_ref[0, LANES:2 * LANES, :] = lo.T.astype(BF16)

    scale = (IDX_HEADS ** -0.5) * (IDX_HEAD_DIM ** -0.5)
    wi_ref[0] = jnp.where(low, 0.0, ikw * scale)


def _idx_prep_call(proj, lnw_p, lnb_p, b, l, tr=512):
    nr = l // tr
    return pl.pallas_call(
        _idx_prep_kernel,
        grid=(b, nr),
        in_specs=[pl.BlockSpec((tr, 512), lambda bi, i: (bi * nr + i, COL_IQ // 512)),
                  pl.BlockSpec((tr, LANES), lambda bi, i: (bi * nr + i, COL_IKW // LANES)),
                  pl.BlockSpec((1, LANES), lambda bi, i: (0, 0)),
                  pl.BlockSpec((1, LANES), lambda bi, i: (0, 0))],
        out_specs=[pl.BlockSpec((1, IDX_HEADS, tr, 2 * LANES), lambda bi, i: (bi, 0, i, 0)),
                   pl.BlockSpec((1, 2 * LANES, tr), lambda bi, i: (bi, 0, i)),
                   pl.BlockSpec((1, tr, LANES), lambda bi, i: (bi, i, 0))],
        out_shape=[jax.ShapeDtypeStruct((b, IDX_HEADS, l, 2 * LANES), BF16),
                   jax.ShapeDtypeStruct((b, 2 * LANES, l), BF16),
                   jax.ShapeDtypeStruct((b, l, LANES), F32)],
        compiler_params=_cparams(("arbitrary", "arbitrary")),
        name="idx_prep",
    )(proj, proj, lnw_p, lnb_p)


def _key_to_f32(key):
    bits = jnp.where(key < 0, key ^ jnp.int32(-2 ** 31), ~key)
    return lax.bitcast_convert_type(bits, F32)


def _dsa_kernel(iqs_ref, kit_ref, wi_ref, q_ref, kt_ref, v_ref, gate_ref, tri_ref, o_ref,
                sc_ref, m_ref, l_ref, acc_ref, *, tq, tk, topk):
    i = pl.program_id(1)
    q0 = i * tq
    nkt = (q0 + tq + tk - 1) // tk
    qpos = q0 + lax.broadcasted_iota(jnp.int32, (tq, 1), 0)
    lane_k = lax.broadcasted_iota(jnp.int32, (tq, tk), 1)

    wi = wi_ref[0]
    wb = [jnp.broadcast_to(wi[:, IDX_HEAD_DIM + h:IDX_HEAD_DIM + h + 1], (tq, tk)) for h in range(IDX_HEADS)]
    iqs = iqs_ref[0].reshape(IDX_HEADS * tq, 2 * LANES)

    def score_tile(j, carry):
        k0 = pl.multiple_of(j * tk, tk)
        s = jnp.dot(iqs, kit_ref[0, :, pl.ds(k0, tk)], preferred_element_type=F32)
        tot = jnp.zeros((tq, tk), F32)
        for h in range(IDX_HEADS):
            tot = tot + jnp.maximum(s[h * tq:(h + 1) * tq], 0.0) * wb[h]
        sc_ref[:, pl.ds(k0, tk)] = jnp.where(k0 + lane_k <= qpos, tot, -jnp.inf)
        return carry

    lax.fori_loop(0, nkt, score_tile, 0)

    def count(pred):
        def body(j, c):
            k0 = pl.multiple_of(j * tk, tk)
            hit = pred(sc_ref[:, pl.ds(k0, tk)]).astype(jnp.int32)
            part = hit[:, 0:LANES]
            for t in range(1, tk // LANES):
                part = part + hit[:, t * LANES:(t + 1) * LANES]
            return c + part
        c = lax.fori_loop(0, nkt, body, jnp.zeros((tq, LANES), jnp.int32))
        return jnp.sum(c, axis=-1, keepdims=True)

    def search(b, prefix):
        cand = prefix | (jnp.int32(1) << (31 - b))
        cand_f = _key_to_f32(cand)
        cnt = count(lambda s: s >= cand_f)
        return jnp.where(cnt >= topk, cand, prefix)

    prefix = lax.fori_loop(0, 32, search, jnp.zeros((tq, 1), jnp.int32))
    thr = jnp.where(qpos < topk, LOWEST, _key_to_f32(prefix))
    need = topk - count(lambda s: s > thr)

    m_ref[...] = jnp.full(m_ref.shape, NEG, F32)
    l_ref[...] = jnp.zeros(l_ref.shape, F32)
    acc_ref[...] = jnp.zeros(acc_ref.shape, F32)
    lane_q = lax.broadcasted_iota(jnp.int32, (tq, LANES), 1)
    low = lane_q < ATT_HEAD_DIM
    zq = jnp.zeros((tq, LANES), BF16)

    def attend(j, eq_seen):
        k0 = pl.multiple_of(j * tk, tk)
        sc = sc_ref[:, pl.ds(k0, tk)]
        eq = sc == thr
        before = jnp.dot(eq.astype(BF16), tri_ref[...], preferred_element_type=F32)
        sel = (sc > thr) | (eq & (before + eq_seen < need.astype(F32)))
        for h in range(ATT_HEADS):
            pr = h // 2
            qp = q_ref[0, :, pr * LANES:(pr + 1) * LANES]
            qh = jnp.where(low if h % 2 == 0 else ~low, qp, zq)
            s = jnp.dot(qh, kt_ref[0, pr * LANES:(pr + 1) * LANES, pl.ds(k0, tk)], preferred_element_type=F32)
            s = jnp.where(sel, s, NEG)
            m_old = m_ref[h]
            m_new = jnp.maximum(m_old, jnp.max(s, axis=-1, keepdims=True))
            alpha = jnp.exp(m_old - m_new)
            p = jnp.where(sel, jnp.exp(s - m_new), 0.0)
            l_ref[h] = alpha * l_ref[h] + jnp.sum(p, axis=-1, keepdims=True)
            pv = jnp.dot(p.astype(BF16), v_ref[0, pl.ds(k0, tk), pr * LANES:(pr + 1) * LANES],
                         preferred_element_type=F32)
            acc_ref[h] = alpha * acc_ref[h] + pv
            m_ref[h] = m_new
        return eq_seen + jnp.sum(eq.astype(F32), axis=-1, keepdims=True)

    lax.fori_loop(0, nkt, attend, jnp.zeros((tq, 1), F32))

    for pr in range(ATT_HEADS // 2):
        oe = acc_ref[2 * pr] / l_ref[2 * pr]
        oo = acc_ref[2 * pr + 1] / l_ref[2 * pr + 1]
        gate = gate_ref[:, pr * LANES:(pr + 1) * LANES]
        o_ref[0, :, pr * LANES:(pr + 1) * LANES] = jnp.where(low, oe, oo) * (gate * jax.nn.sigmoid(gate))


def _dsa_call(iqs, kit, wi, q, kt, v, proj, tri, b, l, tq=128, tk=256):
    nq = l // tq
    topk = min(TOPK_MAX, l // 4)
    kern = functools.partial(_dsa_kernel, tq=tq, tk=tk, topk=topk)
    return pl.pallas_call(
        kern,
        grid=(b, nq),
        in_specs=[pl.BlockSpec((1, IDX_HEADS, tq, 2 * LANES), lambda bi, i: (bi, 0, i, 0)),
                  pl.BlockSpec((1, 2 * LANES, l), lambda bi, i: (bi, 0, 0)),
                  pl.BlockSpec((1, tq, LANES), lambda bi, i: (bi, i, 0)),
                  pl.BlockSpec((1, tq, ATT_WIDTH), lambda bi, i: (bi, i, 0)),
                  pl.BlockSpec((1, ATT_WIDTH, l), lambda bi, i: (bi, 0, 0)),
                  pl.BlockSpec((1, l, ATT_WIDTH), lambda bi, i: (bi, 0, 0)),
                  pl.BlockSpec((tq, ATT_WIDTH), lambda bi, i: (bi * nq + i, COL_AG // ATT_WIDTH)),
                  pl.BlockSpec((tk, tk), lambda bi, i: (0, 0))],
        out_specs=pl.BlockSpec((1, tq, ATT_WIDTH), lambda bi, i: (bi, i, 0)),
        out_shape=jax.ShapeDtypeStruct((b, l, ATT_WIDTH), F32),
        scratch_shapes=[pltpu.VMEM((tq, l), F32),
                        pltpu.VMEM((ATT_HEADS, tq, 1), F32),
                        pltpu.VMEM((ATT_HEADS, tq, 1), F32),
                        pltpu.VMEM((ATT_HEADS, tq, LANES), F32)],
        compiler_params=_cparams(("arbitrary", "arbitrary")),
        name="dsa",
    )(iqs, kit, wi, q, kt, v, proj, tri)


HIGHEST = lax.Precision.HIGHEST


def _dot_hp(a, b):
    return jnp.dot(a, b, precision=HIGHEST, preferred_element_type=F32)


def _dot_nt_hp(a, b):
    return lax.dot_general(a, b, (((1,), (1,)), ((), ())), precision=HIGHEST, preferred_element_type=F32)


def _dn_kernel(dq_ref, dk_ref, dv_ref, dz_ref, dba_ref, cq_ref, ck_ref, cv_ref, avec_ref, bvec_ref, nw_ref,
               o_ref, ext_ref, state_ref, *, rb):
    h = pl.program_id(1)
    step = pl.program_id(2)
    halo = 8

    @pl.when(step == 0)
    def _():
        ext_ref[:, 0:halo, :] = jnp.zeros((3, halo, DN_HEAD_DIM), F32)
        state_ref[...] = jnp.zeros(state_ref.shape, F32)

    def conv_silu(idx, src_ref, w_ref):
        ext_ref[idx, halo:halo + rb, :] = src_ref[...]
        y = jnp.zeros((rb, DN_HEAD_DIM), F32)
        for j in range(CONV_KERNEL):
            off = halo - (CONV_KERNEL - 1) + j
            y = y + ext_ref[idx, off:off + rb, :] * w_ref[j:j + 1, :]
        ext_ref[idx, 0:halo, :] = ext_ref[idx, rb:rb + halo, :]
        return y * jax.nn.sigmoid(y)

    def l2n(t):
        return t * lax.rsqrt(jnp.sum(t * t, axis=-1, keepdims=True) + EPS)

    qa = l2n(conv_silu(0, dq_ref, cq_ref)) * (DN_HEAD_DIM ** -0.5)
    ka = l2n(conv_silu(1, dk_ref, ck_ref))
    va = conv_silu(2, dv_ref, cv_ref)

    dba = dba_ref[...]
    lane = lax.broadcasted_iota(jnp.int32, (rb, LANES), 1)
    beta_all = jax.nn.sigmoid(dba)
    xg = dba + bvec_ref[...]
    softplus = jnp.maximum(xg, 0.0) + jnp.log1p(jnp.exp(-jnp.abs(xg)))
    g_all = -jnp.exp(avec_ref[...]) * softplus
    beta = jnp.sum(jnp.where(lane == h, beta_all, 0.0), axis=-1, keepdims=True)
    g = jnp.sum(jnp.where(lane == DN_HEADS + h, g_all, 0.0), axis=-1, keepdims=True)

    r = lax.broadcasted_iota(jnp.int32, (CHUNK, CHUNK), 0)
    c = lax.broadcasted_iota(jnp.int32, (CHUNK, CHUNK), 1)
    lower = r >= c
    strict = r > c
    tril = lower.astype(F32)

    for ci in range(rb // CHUNK):
        sl = slice(ci * CHUNK, (ci + 1) * CHUNK)
        qc, kc, vc, bc = qa[sl], ka[sl], va[sl], beta[sl]
        gcb = _dot_hp(tril, jnp.broadcast_to(g[sl], (CHUNK, LANES)))
        gc_row = gcb.T[0:CHUNK, :]
        decay = jnp.exp(jnp.where(lower, gcb[:, 0:CHUNK] - gc_row, NEG))
        k_beta = kc * bc
        v_beta = vc * bc
        nmat = -jnp.where(strict, _dot_nt_hp(k_beta, kc) * decay, 0.0)
        sol = jnp.concatenate([v_beta, k_beta * jnp.exp(gcb)], axis=-1)
        for it in range(6):
            sol = sol + _dot_hp(nmat, sol)
            if it < 5:
                nmat = _dot_hp(nmat, nmat)
        u, w = sol[:, 0:DN_HEAD_DIM], sol[:, DN_HEAD_DIM:]
        intra = jnp.where(lower, _dot_nt_hp(qc, kc) * decay, 0.0)
        state = state_ref[...]
        v_new = u - _dot_hp(w, state)
        o_c = _dot_hp(qc * jnp.exp(gcb), state) + _dot_hp(intra, v_new)
        g_last = gcb[CHUNK - 1:CHUNK, :]
        kdec = kc * jnp.exp(g_last - gcb)
        state_ref[...] = state * jnp.exp(g_last) + _dot_hp(kdec.T, v_new)
        z = dz_ref[sl, :]
        on = o_c * lax.rsqrt(jnp.mean(o_c * o_c, axis=-1, keepdims=True) + EPS) * nw_ref[...]
        o_ref[sl, :] = on * (z * jax.nn.sigmoid(z))


def _dn_call(proj, conv_w, avec, bvec, norm_w, b, l, rb=512):
    nr = l // rb
    kern = functools.partial(_dn_kernel, rb=rb)

    def col(base):
        return lambda bi, h, i: (bi * nr + i, base // DN_HEAD_DIM + h)

    def cw(group):
        return lambda bi, h, i: (0, group * DN_HEADS + h)

    return pl.pallas_call(
        kern,
        grid=(b, DN_HEADS, nr),
        in_specs=[pl.BlockSpec((rb, DN_HEAD_DIM), col(COL_DQ)),
                  pl.BlockSpec((rb, DN_HEAD_DIM), col(COL_DK)),
                  pl.BlockSpec((rb, DN_HEAD_DIM), col(COL_DV)),
                  pl.BlockSpec((rb, DN_HEAD_DIM), col(COL_DZ)),
                  pl.BlockSpec((rb, LANES), lambda bi, h, i: (bi * nr + i, COL_DBA // LANES)),
                  pl.BlockSpec((CONV_KERNEL, DN_HEAD_DIM), cw(0)),
                  pl.BlockSpec((CONV_KERNEL, DN_HEAD_DIM), cw(1)),
                  pl.BlockSpec((CONV_KERNEL, DN_HEAD_DIM), cw(2)),
                  pl.BlockSpec((1, LANES), lambda bi, h, i: (0, 0)),
                  pl.BlockSpec((1, LANES), lambda bi, h, i: (0, 0)),
                  pl.BlockSpec((1, DN_HEAD_DIM), lambda bi, h, i: (0, 0))],
        out_specs=pl.BlockSpec((rb, DN_HEAD_DIM), lambda bi, h, i: (bi * nr + i, h)),
        out_shape=jax.ShapeDtypeStruct((b * l, DN_WIDTH), F32),
        scratch_shapes=[pltpu.VMEM((3, rb + 8, DN_HEAD_DIM), F32),
                        pltpu.VMEM((DN_HEAD_DIM, DN_HEAD_DIM), F32)],
        compiler_params=_cparams(("arbitrary", "arbitrary", "arbitrary")),
        name="deltanet",
    )(proj, proj, proj, proj, proj, conv_w, conv_w, conv_w, avec, bvec, norm_w)


def _out_kernel(x_ref, oa_ref, od_ref, wa_ref, wd_ref, o_ref):
    acc = jnp.dot(oa_ref[...].astype(BF16), wa_ref[...], preferred_element_type=F32)
    acc = acc + jnp.dot(od_ref[...].astype(BF16), wd_ref[...], preferred_element_type=F32)
    o_ref[...] = x_ref[...] + acc


def _out_call(x2, oa, od, wa, wd, tm=512):
    n = x2.shape[0]
    return pl.pallas_call(
        _out_kernel,
        grid=(n // tm,),
        in_specs=[pl.BlockSpec((tm, D_MODEL), lambda i: (i, 0)),
                  pl.BlockSpec((tm, ATT_WIDTH), lambda i: (i, 0)),
                  pl.BlockSpec((tm, DN_WIDTH), lambda i: (i, 0)),
                  pl.BlockSpec((ATT_WIDTH, D_MODEL), lambda i: (0, 0)),
                  pl.BlockSpec((DN_WIDTH, D_MODEL), lambda i: (0, 0))],
        out_specs=pl.BlockSpec((tm, D_MODEL), lambda i: (i, 0)),
        out_shape=jax.ShapeDtypeStruct((n, D_MODEL), F32),
        compiler_params=_cparams(("arbitrary",)),
        name="out_proj",
    )(x2, oa, od, wa, wd)


def _layer(h, ln_w, w_in, attn_q_norm_w, attn_k_norm_w, idx_k_norm_w, idx_k_norm_b,
           dn_conv_w, dn_a_log, dn_dt_bias, dn_norm_w, w_out):
    b, l, _ = h.shape
    x2 = h.reshape(b * l, D_MODEL)

    n_ikw = IDX_HEAD_DIM + IDX_HEADS
    w_pad = jnp.concatenate(
        [w_in[:, :COL_IKW + n_ikw], jnp.zeros((D_MODEL, LANES - n_ikw), F32),
         w_in[:, COL_IKW + n_ikw:], jnp.zeros((D_MODEL, LANES - 2 * DN_HEADS), F32)], axis=1).astype(BF16)
    grp = jnp.arange(ATT_WIDTH) // ATT_HEAD_DIM
    gmat = (grp[:, None] == grp[None, :]).astype(BF16)
    wq_t = jnp.tile(attn_q_norm_w, ATT_HEADS)[None, :]
    wk_t = jnp.tile(attn_k_norm_w, ATT_HEADS)[None, :]
    lnw_p = jnp.pad(idx_k_norm_w, (0, LANES - IDX_HEAD_DIM))[None, :]
    lnb_p = jnp.pad(idx_k_norm_b, (0, LANES - IDX_HEAD_DIM))[None, :]
    avec = jnp.pad(dn_a_log, (DN_HEADS, LANES - 2 * DN_HEADS))[None, :]
    bvec = jnp.pad(dn_dt_bias, (DN_HEADS, LANES - 2 * DN_HEADS))[None, :]
    tk = 256
    tri = (jnp.arange(tk)[:, None] < jnp.arange(tk)[None, :]).astype(BF16)

    proj = _proj_call(x2, ln_w[None, :], w_pad)
    q, kt, v = _attn_prep_call(proj, gmat, wq_t, wk_t, b, l)
    iqs, kit, wi = _idx_prep_call(proj, lnw_p, lnb_p, b, l)
    o_a = _dsa_call(iqs, kit, wi, q, kt, v, proj, tri, b, l, tk=tk)
    o_d = _dn_call(proj, dn_conv_w, avec, bvec, dn_norm_w[None, :], b, l)
    out = _out_call(x2, o_a.reshape(b * l, ATT_WIDTH), o_d,
                    w_out[:ATT_WIDTH].astype(BF16), w_out[ATT_WIDTH:].astype(BF16))
    return out.reshape(b, l, D_MODEL)


def kernel(x, ln_w, w_in, attn_q_norm_w, attn_k_norm_w, idx_k_norm_w, idx_k_norm_b, dn_conv_w, dn_A_log,
           dn_dt_bias, dn_norm_w, w_out):
    h = x
    for layer in range(ln_w.shape[0]):
        h = _layer(h, ln_w[layer], w_in[layer], attn_q_norm_w[layer], attn_k_norm_w[layer],
                   idx_k_norm_w[layer], idx_k_norm_b[layer], dn_conv_w[layer], dn_A_log[layer],
                   dn_dt_bias[layer], dn_norm_w[layer], w_out[layer])
    return h
```

```python
import functools

import jax
import jax.numpy as jnp
from jax import lax
from jax.experimental import pallas as pl
from jax.experimental.pallas import tpu as pltpu

F32 = jnp.float32
BF16 = jnp.bfloat16

D_MODEL = 1024
ATT_HEADS = 8
ATT_HEAD_DIM = 64
ATT_WIDTH = ATT_HEADS * ATT_HEAD_DIM
IDX_HEADS = 8
IDX_HEAD_DIM = 64
TOPK_MAX = 256
DN_HEADS = 4
DN_HEAD_DIM = 128
DN_WIDTH = DN_HEADS * DN_HEAD_DIM
CONV_KERNEL = 4
CHUNK = 64
EPS = 1e-6
NEG = -1e30
LANES = 128
SUBLANES = 8
LOWEST = -3.0e38
LOG2E = 1.4426950408889634
LOGIT_SAFE = 60.0
BF16_SLACK = 1.02

COL_AQ, COL_AK, COL_AV, COL_AG = 0, 512, 1024, 1536
COL_IQ = 2048
COL_IKW = 2560
COL_DQ, COL_DK, COL_DV, COL_DZ = 2688, 3200, 3712, 4224
COL_DBA = 4736
D_PAD = 4864
N_SPLIT = 2
IDX_K = 4 * IDX_HEAD_DIM

VMEM_LIMIT = 60 * 1024 * 1024


def _cparams(sem, flags=None):
    return pltpu.CompilerParams(dimension_semantics=sem, vmem_limit_bytes=VMEM_LIMIT, flags=flags)


def _proj_kernel(x_ref, lnw_ref, w_ref, o_ref):
    xf = x_ref[...]
    ms = jnp.mean(xf * xf, axis=-1, keepdims=True)
    hn = xf * lax.rsqrt(ms + EPS) * lnw_ref[...]
    o_ref[...] = jnp.dot(hn.astype(BF16), w_ref[...], preferred_element_type=F32)


def _proj_call(x2, ln_w, w_pad, tm=512):
    n = x2.shape[0]
    tn = D_PAD // N_SPLIT
    return pl.pallas_call(
        _proj_kernel,
        grid=(N_SPLIT, n // tm),
        in_specs=[pl.BlockSpec((tm, D_MODEL), lambda j, i: (i, 0)),
                  pl.BlockSpec((1, D_MODEL), lambda j, i: (0, 0)),
                  pl.BlockSpec((D_MODEL, tn), lambda j, i: (0, j))],
        out_specs=pl.BlockSpec((tm, tn), lambda j, i: (i, j)),
        out_shape=jax.ShapeDtypeStruct((n, D_PAD), F32),
        compiler_params=_cparams(("arbitrary", "arbitrary")),
        name="proj",
    )(x2, ln_w, w_pad)


def _group_sumsq(x, g):
    sq = x * x
    hi = sq.astype(BF16)
    lo = (sq - hi.astype(F32)).astype(BF16)
    return (jnp.dot(hi, g, preferred_element_type=F32) + jnp.dot(lo, g, preferred_element_type=F32))


def _attn_prep_kernel(aq_ref, ak_ref, av_ref, g_ref, wq_ref, wk_ref, qt_ref, k_ref, vt_ref):
    g = g_ref[...]
    aq = aq_ref[...]
    ak = ak_ref[...]
    inv_d = 1.0 / ATT_HEAD_DIM
    qn = aq * lax.rsqrt(_group_sumsq(aq, g) * inv_d + EPS) * wq_ref[...]
    kn = ak * lax.rsqrt(_group_sumsq(ak, g) * inv_d + EPS) * wk_ref[...]
    qt_ref[0] = (qn * (ATT_HEAD_DIM ** -0.5 * LOG2E)).T.astype(BF16)
    k_ref[0] = kn.astype(BF16)
    vt_ref[0] = av_ref[...].T.astype(BF16)


def _attn_prep_call(proj, gmat, wq_t, wk_t, b, l, tr=512):
    nr = l // tr
    wblk = ATT_WIDTH
    return pl.pallas_call(
        _attn_prep_kernel,
        grid=(b, nr),
        in_specs=[pl.BlockSpec((tr, wblk), lambda bi, i: (bi * nr + i, COL_AQ // wblk)),
                  pl.BlockSpec((tr, wblk), lambda bi, i: (bi * nr + i, COL_AK // wblk)),
                  pl.BlockSpec((tr, wblk), lambda bi, i: (bi * nr + i, COL_AV // wblk)),
                  pl.BlockSpec((wblk, wblk), lambda bi, i: (0, 0)),
                  pl.BlockSpec((1, wblk), lambda bi, i: (0, 0)),
                  pl.BlockSpec((1, wblk), lambda bi, i: (0, 0))],
        out_specs=[pl.BlockSpec((1, wblk, tr), lambda bi, i: (bi, 0, i)),
                   pl.BlockSpec((1, tr, wblk), lambda bi, i: (bi, i, 0)),
                   pl.BlockSpec((1, wblk, tr), lambda bi, i: (bi, 0, i))],
        out_shape=[jax.ShapeDtypeStruct((b, wblk, l), BF16),
                   jax.ShapeDtypeStruct((b, l, wblk), BF16),
                   jax.ShapeDtypeStruct((b, wblk, l), BF16)],
        compiler_params=_cparams(("arbitrary", "arbitrary")),
        name="attn_prep",
    )(proj, proj, proj, gmat, wq_t, wk_t)


def _hi_lo(x):
    hi = x.astype(BF16).astype(F32)
    return hi, x - hi


def _idx_prep_kernel(iq_ref, ikw_ref, lnw_ref, lnb_ref, iqt_ref, kidx_ref, wit_ref):
    tr = iq_ref.shape[0]
    lane = lax.broadcasted_iota(jnp.int32, (tr, LANES), 1)
    low = lane < IDX_HEAD_DIM

    for j in range(IDX_HEADS // 2):
        d = iq_ref[:, j * LANES:(j + 1) * LANES]
        r = pltpu.roll(d, IDX_HEAD_DIM, 1)
        for half, dup in enumerate((jnp.where(low, d, r), jnp.where(low, r, d))):
            hi, lo = _hi_lo(dup)
            h = 2 * j + half
            iqt_ref[0, h, 0:LANES, :] = jnp.where(low, hi, lo).T.astype(BF16)
            iqt_ref[0, h, LANES:2 * LANES, :] = jnp.where(low, hi, 0.0).T.astype(BF16)

    ikw = ikw_ref[...]
    inv_d = 1.0 / IDX_HEAD_DIM
    mu = jnp.sum(jnp.where(low, ikw, 0.0), axis=-1, keepdims=True) * inv_d
    cen = jnp.where(low, ikw - mu, 0.0)
    var = jnp.sum(cen * cen, axis=-1, keepdims=True) * inv_d
    kn = jnp.where(low, cen * lax.rsqrt(var + EPS) * lnw_ref[...] + lnb_ref[...], 0.0)
    hi, lo = _hi_lo(kn)
    kidx_ref[0, :, 0:LANES] = (hi + pltpu.roll(hi, IDX_HEAD_DIM, 1)).astype(BF16)
    kidx_ref[0, :, LANES:2 * LANES] = lo.astype(BF16)

    scale = (IDX_HEADS ** -0.5) * (IDX_HEAD_DIM ** -0.5)
    wit_ref[0] = (ikw * scale).T[IDX_HEAD_DIM:IDX_HEAD_DIM + IDX_HEADS, :]


def _idx_prep_call(proj, lnw_p, lnb_p, b, l, tr=512):
    nr = l // tr
    return pl.pallas_call(
        _idx_prep_kernel,
        grid=(b, nr),
        in_specs=[pl.BlockSpec((tr, 512), lambda bi, i: (bi * nr + i, COL_IQ // 512)),
                  pl.BlockSpec((tr, LANES), lambda bi, i: (bi * nr + i, COL_IKW // LANES)),
                  pl.BlockSpec((1, LANES), lambda bi, i: (0, 0)),
                  pl.BlockSpec((1, LANES), lambda bi, i: (0, 0))],
        out_specs=[pl.BlockSpec((1, IDX_HEADS, IDX_K, tr), lambda bi, i: (bi, 0, 0, i)),
                   pl.BlockSpec((1, tr, IDX_K), lambda bi, i: (bi, i, 0)),
                   pl.BlockSpec((1, IDX_HEADS, tr), lambda bi, i: (bi, 0, i))],
        out_shape=[jax.ShapeDtypeStruct((b, IDX_HEADS, IDX_K, l), BF16),
                   jax.ShapeDtypeStruct((b, l, IDX_K), BF16),
                   jax.ShapeDtypeStruct((b, IDX_HEADS, l), F32)],
        compiler_params=_cparams(("arbitrary", "arbitrary")),
        name="idx_prep",
    )(proj, proj, lnw_p, lnb_p)


def _key_to_f32(key):
    bits = jnp.where(key < 0, key ^ jnp.int32(-2 ** 31), ~key)
    return lax.bitcast_convert_type(bits, F32)


def _dsa_kernel(iqt_ref, wit_ref, qt_ref, gate_ref, tri_ref, kidx_ref, k_ref, vt_ref, o_ref,
                sc_ref, qh_ref, acc_ref, s_ref=None, *, tq, tk, topk, online_max):
    i = pl.program_id(1)
    q0 = i * tq
    nkt = (q0 + tq + tk - 1) // tk
    qpos = q0 + lax.broadcasted_iota(jnp.int32, (1, tq), 1)
    krow = lax.broadcasted_iota(jnp.int32, (tk, tq), 0)

    def score_tile(j, carry):
        k0 = pl.multiple_of(j * tk, tk)
        kk = kidx_ref[0, pl.ds(k0, tk), :]
        tot = jnp.zeros((tk, tq), F32)
        for h in range(IDX_HEADS):
            s = jnp.dot(kk, iqt_ref[0, h], preferred_element_type=F32)
            tot = tot + jnp.maximum(s, 0.0) * wit_ref[0, h:h + 1, :]
        sc_ref[pl.ds(k0, tk), :] = jnp.where(k0 + krow <= qpos, tot, -jnp.inf)
        return carry

    lax.fori_loop(0, nkt, score_tile, 0)

    def count(pred):
        def body(j, c):
            k0 = pl.multiple_of(j * tk, tk)
            hit = pred(sc_ref[pl.ds(k0, tk), :]).astype(jnp.int32)
            return c + jnp.sum(hit.reshape(tk // SUBLANES, SUBLANES, tq), axis=0)
        c = lax.fori_loop(0, nkt, body, jnp.zeros((SUBLANES, tq), jnp.int32))
        return jnp.sum(c, axis=0, keepdims=True)

    def search(b, prefix):
        cand = prefix | (jnp.int32(1) << (31 - b))
        cand_f = _key_to_f32(cand)
        cnt = count(lambda s: s >= cand_f)
        return jnp.where(cnt >= topk, cand, prefix)

    prefix = lax.fori_loop(0, 32, search, jnp.zeros((1, tq), jnp.int32))
    thr = jnp.where(qpos < topk, LOWEST, _key_to_f32(prefix))
    need = topk - count(lambda s: s > thr)
    n_eq = count(lambda s: s == thr)
    any_cut_tie = jnp.max(jnp.where(n_eq > need, 1, 0)) > 0
    need_f = need.astype(F32)

    acc_ref[...] = jnp.zeros(acc_ref.shape, F32)
    top_half = lax.broadcasted_iota(jnp.int32, (LANES, tq), 0) < ATT_HEAD_DIM
    for h in range(ATT_HEADS):
        pr = h // 2
        qp = qt_ref[0, pr * LANES:(pr + 1) * LANES, :]
        qh_ref[h] = jnp.where(top_half if h % 2 == 0 else ~top_half, qp, jnp.zeros_like(qp))

    def bias_plain(sc, eq_seen):
        return jnp.where(sc >= thr, 0.0, NEG), eq_seen

    def bias_ties(sc, eq_seen):
        eq = sc == thr
        before = jnp.dot(tri_ref[...], eq.astype(BF16), preferred_element_type=F32)
        sel = (sc > thr) | (eq & (before + eq_seen < need_f))
        return jnp.where(sel, 0.0, NEG), eq_seen + jnp.sum(eq.astype(F32), axis=0, keepdims=True)

    def to_bias(j, eq_seen):
        k0 = pl.multiple_of(j * tk, tk)
        bias, eq_seen = lax.cond(any_cut_tie, bias_ties, bias_plain, sc_ref[pl.ds(k0, tk), :], eq_seen)
        sc_ref[pl.ds(k0, tk), :] = bias
        return eq_seen

    lax.fori_loop(0, nkt, to_bias, jnp.zeros((1, tq), F32))
    npair = (nkt + 1) // 2

    @pl.when(nkt % 2 == 1)
    def _():
        sc_ref[pl.ds(pl.multiple_of(nkt * tk, tk), tk), :] = jnp.full((tk, tq), NEG, F32)

    def finish(l_fin):
        rows = []
        for h in range(ATT_HEADS):
            r0 = (h % 2) * ATT_HEAD_DIM
            rows.append(acc_ref[h, r0:r0 + ATT_HEAD_DIM, :] / l_fin[h:h + 1, :])
        gate = gate_ref[...]
        o_ref[0] = jnp.concatenate(rows, axis=0).T * (gate * jax.nn.sigmoid(gate))

    if not online_max:
        def stage_logits(j, slot):
            k0 = pl.multiple_of(j * tk, tk)
            bias = sc_ref[pl.ds(k0, tk), :]
            for h in range(ATT_HEADS):
                pr = h // 2
                s_ref[slot, h] = jnp.dot(k_ref[0, pl.ds(k0, tk), pr * LANES:(pr + 1) * LANES], qh_ref[h],
                                         preferred_element_type=F32) + bias

        def consume(j, slot, l_all):
            k0 = pl.multiple_of(j * tk, tk)
            ls = []
            for h in range(ATT_HEADS):
                pr = h // 2
                p = jnp.exp2(s_ref[slot, h])
                ls.append(jnp.sum(p, axis=0, keepdims=True))
                acc_ref[h] += jnp.dot(vt_ref[0, pr * LANES:(pr + 1) * LANES, pl.ds(k0, tk)], p.astype(BF16),
                                      preferred_element_type=F32)
            return l_all + jnp.concatenate(ls, axis=0)

        def attend_bounded(jp, l_all):
            stage_logits(2 * jp + 1, 1)
            l_all = consume(2 * jp, 0, l_all)
            stage_logits(jnp.minimum(2 * jp + 2, 2 * npair - 2), 0)
            return consume(2 * jp + 1, 1, l_all)

        stage_logits(0, 0)
        finish(lax.fori_loop(0, npair, attend_bounded, jnp.zeros((ATT_HEADS, tq), F32)))
        return

    def logits_pass(j, slot):
        k0 = pl.multiple_of(j * tk, tk)
        bias = sc_ref[pl.ds(k0, tk), :]
        mx = []
        for h in range(ATT_HEADS):
            pr = h // 2
            s = jnp.dot(k_ref[0, pl.ds(k0, tk), pr * LANES:(pr + 1) * LANES], qh_ref[h],
                        preferred_element_type=F32) + bias
            s_ref[slot, h] = s
            mx.append(jnp.max(s, axis=0, keepdims=True))
        return jnp.concatenate(mx, axis=0)

    def value_pass(j, slot, m_all, l_all, mx):
        k0 = pl.multiple_of(j * tk, tk)
        m_new = jnp.maximum(m_all, mx)
        alpha = jnp.exp2(m_all - m_new)
        ls = []
        for h in range(ATT_HEADS):
            pr = h // 2
            p = jnp.exp2(s_ref[slot, h] - m_new[h:h + 1, :])
            ls.append(jnp.sum(p, axis=0, keepdims=True))
            pv = jnp.dot(vt_ref[0, pr * LANES:(pr + 1) * LANES, pl.ds(k0, tk)], p.astype(BF16),
                         preferred_element_type=F32)
            acc_ref[h] = alpha[h:h + 1, :] * acc_ref[h] + pv
        return m_new, alpha * l_all + jnp.concatenate(ls, axis=0)

    def attend(jp, carry):
        m_all, l_all, mx0 = carry
        mx1 = logits_pass(2 * jp + 1, 1)
        m_all, l_all = value_pass(2 * jp, 0, m_all, l_all, mx0)
        mx0 = logits_pass(jnp.minimum(2 * jp + 2, 2 * npair - 2), 0)
        m_all, l_all = value_pass(2 * jp + 1, 1, m_all, l_all, mx1)
        return m_all, l_all, mx0

    init = (jnp.full((ATT_HEADS, tq), NEG, F32), jnp.zeros((ATT_HEADS, tq), F32), logits_pass(0, 0))
    finish(lax.fori_loop(0, npair, attend, init)[1])


def _dsa_call(iqt, wit, qt, proj, tri, kidx, k, vt, b, l, tq, tk, online_max):
    nq = l // tq
    topk = min(TOPK_MAX, l // 4)
    kern = functools.partial(_dsa_kernel, tq=tq, tk=tk, topk=topk, online_max=online_max)
    s_stage = [pltpu.VMEM((2, ATT_HEADS, tk, tq), F32)]
    once = pl.Buffered(1)
    return pl.pallas_call(
        kern,
        grid=(b, nq),
        in_specs=[pl.BlockSpec((1, IDX_HEADS, IDX_K, tq), lambda bi, i: (bi, 0, 0, i)),
                  pl.BlockSpec((1, IDX_HEADS, tq), lambda bi, i: (bi, 0, i)),
                  pl.BlockSpec((1, ATT_WIDTH, tq), lambda bi, i: (bi, 0, i)),
                  pl.BlockSpec((tq, ATT_WIDTH), lambda bi, i: (bi * nq + i, COL_AG // ATT_WIDTH)),
                  pl.BlockSpec((tk, tk), lambda bi, i: (0, 0), pipeline_mode=once),
                  pl.BlockSpec((1, l, IDX_K), lambda bi, i: (bi, 0, 0), pipeline_mode=once),
                  pl.BlockSpec((1, l, ATT_WIDTH), lambda bi, i: (bi, 0, 0), pipeline_mode=once),
                  pl.BlockSpec((1, ATT_WIDTH, l), lambda bi, i: (bi, 0, 0), pipeline_mode=once)],
        out_specs=pl.BlockSpec((1, tq, ATT_WIDTH), lambda bi, i: (bi, i, 0)),
        out_shape=jax.ShapeDtypeStruct((b, l, ATT_WIDTH), F32),
        scratch_shapes=[pltpu.VMEM((l, tq), F32),
                        pltpu.VMEM((ATT_HEADS, LANES, tq), BF16),
                        pltpu.VMEM((ATT_HEADS, LANES, tq), F32)] + s_stage,
        compiler_params=_cparams(("arbitrary", "arbitrary")),
        name="dsa_online_max" if online_max else "dsa",
    )(iqt, wit, qt, proj, tri, kidx, k, vt)


def _mm(a, b):
    return jnp.dot(a.astype(BF16), b.astype(BF16), preferred_element_type=F32)


def _mm_nt(a, b):
    return lax.dot_general(a.astype(BF16), b.astype(BF16), (((1,), (1,)), ((), ())),
                           preferred_element_type=F32)


def _mm_exact_lhs(a01, b):
    hi = b.astype(BF16)
    lo = (b - hi.astype(F32)).astype(BF16)
    a = a01.astype(BF16)
    return jnp.dot(a, hi, preferred_element_type=F32) + jnp.dot(a, lo, preferred_element_type=F32)


def _dn_kernel(dq_ref, dk_ref, dv_ref, dz_ref, dba_ref, cq_ref, ck_ref, cv_ref, avec_ref, bvec_ref, nw_ref,
               o_ref, ext_ref, state_ref, *, rb):
    h = pl.program_id(1)
    step = pl.program_id(2)
    halo = SUBLANES

    @pl.when(step == 0)
    def _():
        ext_ref[:, 0:halo, :] = jnp.zeros((3, halo, DN_HEAD_DIM), F32)
        state_ref[...] = jnp.zeros(state_ref.shape, F32)

    def conv_silu(idx, src_ref, w_ref):
        ext_ref[idx, halo:halo + rb, :] = src_ref[...]
        y = jnp.zeros((rb, DN_HEAD_DIM), F32)
        for j in range(CONV_KERNEL):
            off = halo - (CONV_KERNEL - 1) + j
            y = y + ext_ref[idx, off:off + rb, :] * w_ref[j:j + 1, :]
        ext_ref[idx, 0:halo, :] = ext_ref[idx, rb:rb + halo, :]
        return y * jax.nn.sigmoid(y)

    def l2n(t):
        return t * lax.rsqrt(jnp.sum(t * t, axis=-1, keepdims=True) + EPS)

    qa = l2n(conv_silu(0, dq_ref, cq_ref)) * (DN_HEAD_DIM ** -0.5)
    ka = l2n(conv_silu(1, dk_ref, ck_ref))
    va = conv_silu(2, dv_ref, cv_ref)

    dba = dba_ref[...]
    lane = lax.broadcasted_iota(jnp.int32, (rb, LANES), 1)
    beta_all = jax.nn.sigmoid(dba)
    xg = dba + bvec_ref[...]
    softplus = jnp.maximum(xg, 0.0) + jnp.log1p(jnp.exp(-jnp.abs(xg)))
    g_all = -jnp.exp(avec_ref[...]) * softplus
    beta = jnp.sum(jnp.where(lane == h, beta_all, 0.0), axis=-1, keepdims=True)
    g = jnp.sum(jnp.where(lane == DN_HEADS + h, g_all, 0.0), axis=-1, keepdims=True)

    r = lax.broadcasted_iota(jnp.int32, (CHUNK, CHUNK), 0)
    c = lax.broadcasted_iota(jnp.int32, (CHUNK, CHUNK), 1)
    lower = r >= c
    strict = r > c
    tril = lower.astype(F32)

    for ci in range(rb // CHUNK):
        sl = slice(ci * CHUNK, (ci + 1) * CHUNK)
        qc, kc, vc, bc = qa[sl], ka[sl], va[sl], beta[sl]
        gcb = _mm_exact_lhs(tril, jnp.broadcast_to(g[sl], (CHUNK, LANES)))
        gc_row = gcb.T[0:CHUNK, :]
        decay = jnp.exp(jnp.where(lower, gcb[:, 0:CHUNK] - gc_row, NEG))
        k_beta = kc * bc
        v_beta = vc * bc
        nmat = -jnp.where(strict, _mm_nt(k_beta, kc) * decay, 0.0)
        sol = jnp.concatenate([v_beta, k_beta * jnp.exp(gcb)], axis=-1)
        for it in range(6):
            sol = sol + _mm(nmat, sol)
            if it < 5:
                nmat = _mm(nmat, nmat)
        u, w = sol[:, 0:DN_HEAD_DIM], sol[:, DN_HEAD_DIM:]
        intra = jnp.where(lower, _mm_nt(qc, kc) * decay, 0.0)
        state = state_ref[...]
        v_new = u - _mm(w, state)
        o_c = _mm(qc * jnp.exp(gcb), state) + _mm(intra, v_new)
        g_last = gcb[CHUNK - 1:CHUNK, :]
        kdec = kc * jnp.exp(g_last - gcb)
        state_ref[...] = state * jnp.exp(g_last) + _mm(kdec.T, v_new)
        z = dz_ref[sl, :]
        on = o_c * lax.rsqrt(jnp.mean(o_c * o_c, axis=-1, keepdims=True) + EPS) * nw_ref[...]
        o_ref[sl, :] = on * (z * jax.nn.sigmoid(z))


def _dn_call(proj, conv_w, avec, bvec, norm_w, b, l, rb=512):
    nr = l // rb
    kern = functools.partial(_dn_kernel, rb=rb)

    def col(base):
        return lambda bi, h, i: (bi * nr + i, base // DN_HEAD_DIM + h)

    def cw(group):
        return lambda bi, h, i: (0, group * DN_HEADS + h)

    return pl.pallas_call(
        kern,
        grid=(b, DN_HEADS, nr),
        in_specs=[pl.BlockSpec((rb, DN_HEAD_DIM), col(COL_DQ)),
                  pl.BlockSpec((rb, DN_HEAD_DIM), col(COL_DK)),
                  pl.BlockSpec((rb, DN_HEAD_DIM), col(COL_DV)),
                  pl.BlockSpec((rb, DN_HEAD_DIM), col(COL_DZ)),
                  pl.BlockSpec((rb, LANES), lambda bi, h, i: (bi * nr + i, COL_DBA // LANES)),
                  pl.BlockSpec((CONV_KERNEL, DN_HEAD_DIM), cw(0)),
                  pl.BlockSpec((CONV_KERNEL, DN_HEAD_DIM), cw(1)),
                  pl.BlockSpec((CONV_KERNEL, DN_HEAD_DIM), cw(2)),
                  pl.BlockSpec((1, LANES), lambda bi, h, i: (0, 0)),
                  pl.BlockSpec((1, LANES), lambda bi, h, i: (0, 0)),
                  pl.BlockSpec((1, DN_HEAD_DIM), lambda bi, h, i: (0, 0))],
        out_specs=pl.BlockSpec((rb, DN_HEAD_DIM), lambda bi, h, i: (bi * nr + i, h)),
        out_shape=jax.ShapeDtypeStruct((b * l, DN_WIDTH), F32),
        scratch_shapes=[pltpu.VMEM((3, rb + SUBLANES, DN_HEAD_DIM), F32),
                        pltpu.VMEM((DN_HEAD_DIM, DN_HEAD_DIM), F32)],
        compiler_params=_cparams(("arbitrary", "arbitrary", "arbitrary")),
        name="deltanet",
    )(proj, proj, proj, proj, proj, conv_w, conv_w, conv_w, avec, bvec, norm_w)


def _out_kernel(x_ref, oa_ref, od_ref, wa_ref, wd_ref, o_ref):
    acc = jnp.dot(oa_ref[...].astype(BF16), wa_ref[...], preferred_element_type=F32)
    acc = acc + jnp.dot(od_ref[...].astype(BF16), wd_ref[...], preferred_element_type=F32)
    o_ref[...] = x_ref[...] + acc


def _out_call(x2, oa, od, wa, wd, tm=512):
    n = x2.shape[0]
    return pl.pallas_call(
        _out_kernel,
        grid=(n // tm,),
        in_specs=[pl.BlockSpec((tm, D_MODEL), lambda i: (i, 0)),
                  pl.BlockSpec((tm, ATT_WIDTH), lambda i: (i, 0)),
                  pl.BlockSpec((tm, DN_WIDTH), lambda i: (i, 0)),
                  pl.BlockSpec((ATT_WIDTH, D_MODEL), lambda i: (0, 0)),
                  pl.BlockSpec((DN_WIDTH, D_MODEL), lambda i: (0, 0))],
        out_specs=pl.BlockSpec((tm, D_MODEL), lambda i: (i, 0)),
        out_shape=jax.ShapeDtypeStruct((n, D_MODEL), F32),
        compiler_params=_cparams(("arbitrary",)),
        name="out_proj",
    )(x2, oa, od, wa, wd)


def _layer(h, ln_w, w_in, attn_q_norm_w, attn_k_norm_w, idx_k_norm_w, idx_k_norm_b,
           dn_conv_w, dn_a_log, dn_dt_bias, dn_norm_w, w_out):
    b, l, _ = h.shape
    x2 = h.reshape(b * l, D_MODEL)

    n_ikw = IDX_HEAD_DIM + IDX_HEADS
    w_pad = jnp.concatenate(
        [w_in[:, :COL_IKW + n_ikw], jnp.zeros((D_MODEL, LANES - n_ikw), F32),
         w_in[:, COL_IKW + n_ikw:], jnp.zeros((D_MODEL, LANES - 2 * DN_HEADS), F32)], axis=1).astype(BF16)
    grp = jnp.arange(ATT_WIDTH) // ATT_HEAD_DIM
    gmat = (grp[:, None] == grp[None, :]).astype(BF16)
    wq_t = jnp.tile(attn_q_norm_w, ATT_HEADS)[None, :]
    wk_t = jnp.tile(attn_k_norm_w, ATT_HEADS)[None, :]
    lnw_p = jnp.pad(idx_k_norm_w, (0, LANES - IDX_HEAD_DIM))[None, :]
    lnb_p = jnp.pad(idx_k_norm_b, (0, LANES - IDX_HEAD_DIM))[None, :]
    avec = jnp.pad(dn_a_log, (DN_HEADS, LANES - 2 * DN_HEADS))[None, :]
    bvec = jnp.pad(dn_dt_bias, (DN_HEADS, LANES - 2 * DN_HEADS))[None, :]
    tq = min(256, l)
    tk = min(256, l)
    tri = (jnp.arange(tk)[None, :] < jnp.arange(tk)[:, None]).astype(BF16)

    proj = _proj_call(x2, ln_w[None, :], w_pad)
    qt, k, vt = _attn_prep_call(proj, gmat, wq_t, wk_t, b, l)
    iqt, kidx, wit = _idx_prep_call(proj, lnw_p, lnb_p, b, l)
    logit_bound = (ATT_HEAD_DIM ** 0.5 * LOG2E) * jnp.max(jnp.abs(attn_q_norm_w)) * jnp.max(jnp.abs(attn_k_norm_w))
    dsa_args = (iqt, wit, qt, proj, tri, kidx, k, vt)
    o_a = lax.cond(logit_bound * BF16_SLACK < LOGIT_SAFE,
                   lambda *a: _dsa_call(*a, b, l, tq, tk, online_max=False),
                   lambda *a: _dsa_call(*a, b, l, tq, tk, online_max=True), *dsa_args)
    o_d = _dn_call(proj, dn_conv_w, avec, bvec, dn_norm_w[None, :], b, l)
    out = _out_call(x2, o_a.reshape(b * l, ATT_WIDTH), o_d,
                    w_out[:ATT_WIDTH].astype(BF16), w_out[ATT_WIDTH:].astype(BF16))
    return out.reshape(b, l, D_MODEL)


def kernel(x, ln_w, w_in, attn_q_norm_w, attn_k_norm_w, idx_k_norm_w, idx_k_norm_b, dn_conv_w, dn_A_log,
           dn_dt_bias, dn_norm_w, w_out):
    h = x
    for layer in range(ln_w.shape[0]):
        h = _layer(h, ln_w[layer], w_in[layer], attn_q_norm_w[layer], attn_k_norm_w[layer],
                   idx_k_norm_w[layer], idx_k_norm_b[layer], dn_conv_w[layer], dn_A_log[layer],
                   dn_dt_bias[layer], dn_norm_w[layer], w_out[layer])
    return h
```

```python
import functools

import jax
import jax.numpy as jnp
from jax import lax
from jax.experimental import pallas as pl
from jax.experimental.pallas import tpu as pltpu

F32 = jnp.float32
BF16 = jnp.bfloat16

D_MODEL = 1024
ATT_HEADS = 8
ATT_HEAD_DIM = 64
ATT_WIDTH = ATT_HEADS * ATT_HEAD_DIM
IDX_HEADS = 8
IDX_HEAD_DIM = 64
TOPK_MAX = 256
DN_HEADS = 4
DN_HEAD_DIM = 128
DN_WIDTH = DN_HEADS * DN_HEAD_DIM
CONV_KERNEL = 4
CHUNK = 64
EPS = 1e-6
NEG = -1e30
LANES = 128
SUBLANES = 8
LOWEST = -3.0e38
LOG2E = 1.4426950408889634
LOGIT_SAFE = 60.0
BF16_SLACK = 1.02

COL_AQ, COL_AK, COL_AV, COL_AG = 0, 512, 1024, 1536
COL_IQ = 2048
COL_DQ, COL_DK, COL_DV, COL_DZ = 2560, 3072, 3584, 4096
COL_IKW = 4608
COL_DBA = 4736
D_PAD = 4864
N_SPLIT = 2
IDX_K = 4 * IDX_HEAD_DIM

VMEM_LIMIT = 60 * 1024 * 1024


def _cparams(sem, flags=None):
    return pltpu.CompilerParams(dimension_semantics=sem, vmem_limit_bytes=VMEM_LIMIT, flags=flags)


def _proj_kernel(x_ref, lnw_ref, w_ref, o_ref):
    xf = x_ref[...]
    ms = jnp.mean(xf * xf, axis=-1, keepdims=True)
    hn = xf * lax.rsqrt(ms + EPS) * lnw_ref[...]
    o_ref[...] = jnp.dot(hn.astype(BF16), w_ref[...], preferred_element_type=F32)


def _proj_call(x2, ln_w, w_pad, tm=512):
    n = x2.shape[0]
    tn = D_PAD // N_SPLIT
    return pl.pallas_call(
        _proj_kernel,
        grid=(N_SPLIT, n // tm),
        in_specs=[pl.BlockSpec((tm, D_MODEL), lambda j, i: (i, 0)),
                  pl.BlockSpec((1, D_MODEL), lambda j, i: (0, 0)),
                  pl.BlockSpec((D_MODEL, tn), lambda j, i: (0, j))],
        out_specs=pl.BlockSpec((tm, tn), lambda j, i: (i, j)),
        out_shape=jax.ShapeDtypeStruct((n, D_PAD), F32),
        compiler_params=_cparams(("arbitrary", "arbitrary")),
        name="proj",
    )(x2, ln_w, w_pad)


def _group_sumsq(x, g):
    sq = x * x
    hi = sq.astype(BF16)
    lo = (sq - hi.astype(F32)).astype(BF16)
    return (jnp.dot(hi, g, preferred_element_type=F32) + jnp.dot(lo, g, preferred_element_type=F32))


def _attn_prep_kernel(aq_ref, ak_ref, av_ref, g_ref, wq_ref, wk_ref, qt_ref, k_ref, vt_ref):
    g = g_ref[...]
    aq = aq_ref[...]
    ak = ak_ref[...]
    inv_d = 1.0 / ATT_HEAD_DIM
    qn = aq * lax.rsqrt(_group_sumsq(aq, g) * inv_d + EPS) * wq_ref[...]
    kn = ak * lax.rsqrt(_group_sumsq(ak, g) * inv_d + EPS) * wk_ref[...]
    qt_ref[0] = (qn * (ATT_HEAD_DIM ** -0.5 * LOG2E)).T.astype(BF16)
    k_ref[0] = kn.astype(BF16)
    vt_ref[0] = av_ref[...].T.astype(BF16)


def _attn_prep_call(proj, gmat, wq_t, wk_t, b, l, tr=512):
    nr = l // tr
    wblk = ATT_WIDTH
    return pl.pallas_call(
        _attn_prep_kernel,
        grid=(b, nr),
        in_specs=[pl.BlockSpec((tr, wblk), lambda bi, i: (bi * nr + i, COL_AQ // wblk)),
                  pl.BlockSpec((tr, wblk), lambda bi, i: (bi * nr + i, COL_AK // wblk)),
                  pl.BlockSpec((tr, wblk), lambda bi, i: (bi * nr + i, COL_AV // wblk)),
                  pl.BlockSpec((wblk, wblk), lambda bi, i: (0, 0)),
                  pl.BlockSpec((1, wblk), lambda bi, i: (0, 0)),
                  pl.BlockSpec((1, wblk), lambda bi, i: (0, 0))],
        out_specs=[pl.BlockSpec((1, wblk, tr), lambda bi, i: (bi, 0, i)),
                   pl.BlockSpec((1, tr, wblk), lambda bi, i: (bi, i, 0)),
                   pl.BlockSpec((1, wblk, tr), lambda bi, i: (bi, 0, i))],
        out_shape=[jax.ShapeDtypeStruct((b, wblk, l), BF16),
                   jax.ShapeDtypeStruct((b, l, wblk), BF16),
                   jax.ShapeDtypeStruct((b, wblk, l), BF16)],
        compiler_params=_cparams(("arbitrary", "arbitrary")),
        name="attn_prep",
    )(proj, proj, proj, gmat, wq_t, wk_t)


def _hi_lo(x):
    hi = x.astype(BF16).astype(F32)
    return hi, x - hi


def _idx_prep_kernel(iq_ref, ikw_ref, lnw_ref, lnb_ref, iqt_ref, kidx_ref, wit_ref):
    tr = iq_ref.shape[0]
    lane = lax.broadcasted_iota(jnp.int32, (tr, LANES), 1)
    low = lane < IDX_HEAD_DIM

    for j in range(IDX_HEADS // 2):
        d = iq_ref[:, j * LANES:(j + 1) * LANES]
        r = pltpu.roll(d, IDX_HEAD_DIM, 1)
        for half, dup in enumerate((jnp.where(low, d, r), jnp.where(low, r, d))):
            hi, lo = _hi_lo(dup)
            h = 2 * j + half
            iqt_ref[0, h, 0:LANES, :] = jnp.where(low, hi, lo).T.astype(BF16)
            iqt_ref[0, h, LANES:2 * LANES, :] = jnp.where(low, hi, 0.0).T.astype(BF16)

    ikw = ikw_ref[...]
    inv_d = 1.0 / IDX_HEAD_DIM
    mu = jnp.sum(jnp.where(low, ikw, 0.0), axis=-1, keepdims=True) * inv_d
    cen = jnp.where(low, ikw - mu, 0.0)
    var = jnp.sum(cen * cen, axis=-1, keepdims=True) * inv_d
    kn = jnp.where(low, cen * lax.rsqrt(var + EPS) * lnw_ref[...] + lnb_ref[...], 0.0)
    hi, lo = _hi_lo(kn)
    kidx_ref[0, :, 0:LANES] = (hi + pltpu.roll(hi, IDX_HEAD_DIM, 1)).astype(BF16)
    kidx_ref[0, :, LANES:2 * LANES] = lo.astype(BF16)

    scale = (IDX_HEADS ** -0.5) * (IDX_HEAD_DIM ** -0.5)
    wit_ref[0] = (ikw * scale).T[IDX_HEAD_DIM:IDX_HEAD_DIM + IDX_HEADS, :]


def _idx_prep_call(proj, lnw_p, lnb_p, b, l, tr=512):
    nr = l // tr
    return pl.pallas_call(
        _idx_prep_kernel,
        grid=(b, nr),
        in_specs=[pl.BlockSpec((tr, 512), lambda bi, i: (bi * nr + i, COL_IQ // 512)),
                  pl.BlockSpec((tr, LANES), lambda bi, i: (bi * nr + i, COL_IKW // LANES)),
                  pl.BlockSpec((1, LANES), lambda bi, i: (0, 0)),
                  pl.BlockSpec((1, LANES), lambda bi, i: (0, 0))],
        out_specs=[pl.BlockSpec((1, IDX_HEADS, IDX_K, tr), lambda bi, i: (bi, 0, 0, i)),
                   pl.BlockSpec((1, tr, IDX_K), lambda bi, i: (bi, i, 0)),
                   pl.BlockSpec((1, IDX_HEADS, tr), lambda bi, i: (bi, 0, i))],
        out_shape=[jax.ShapeDtypeStruct((b, IDX_HEADS, IDX_K, l), BF16),
                   jax.ShapeDtypeStruct((b, l, IDX_K), BF16),
                   jax.ShapeDtypeStruct((b, IDX_HEADS, l), F32)],
        compiler_params=_cparams(("arbitrary", "arbitrary")),
        name="idx_prep",
    )(proj, proj, lnw_p, lnb_p)


def _key_to_f32(key):
    bits = jnp.where(key < 0, key ^ jnp.int32(-2 ** 31), ~key)
    return lax.bitcast_convert_type(bits, F32)


def _dsa_kernel(iqt_ref, wit_ref, qt_ref, gate_ref, tri_ref, kidx_ref, k_ref, vt_ref, o_ref,
                sc_ref, qh_ref, acc_ref, s_ref=None, *, tq, tk, topk, online_max):
    i = pl.program_id(1)
    q0 = i * tq
    nkt = (q0 + tq + tk - 1) // tk
    qpos = q0 + lax.broadcasted_iota(jnp.int32, (1, tq), 1)
    krow = lax.broadcasted_iota(jnp.int32, (tk, tq), 0)

    def score_tile(j, carry):
        k0 = pl.multiple_of(j * tk, tk)
        kk = kidx_ref[0, pl.ds(k0, tk), :]
        tot = jnp.zeros((tk, tq), F32)
        for h in range(IDX_HEADS):
            s = jnp.dot(kk, iqt_ref[0, h], preferred_element_type=F32)
            tot = tot + jnp.maximum(s, 0.0) * wit_ref[0, h:h + 1, :]
        sc_ref[pl.ds(k0, tk), :] = jnp.where(k0 + krow <= qpos, tot, -jnp.inf)
        return carry

    lax.fori_loop(0, nkt, score_tile, 0)
    npair = (nkt + 1) // 2

    @pl.when(nkt % 2 == 1)
    def _():
        sc_ref[pl.ds(pl.multiple_of(nkt * tk, tk), tk), :] = jnp.full((tk, tq), -jnp.inf, F32)

    def count(pred):
        def body(jp, c):
            k0 = pl.multiple_of(jp * (2 * tk), 2 * tk)
            hit = pred(sc_ref[pl.ds(k0, 2 * tk), :]).astype(jnp.int32)
            return c + jnp.sum(hit.reshape(2 * tk // SUBLANES, SUBLANES, tq), axis=0)
        c = lax.fori_loop(0, npair, body, jnp.zeros((SUBLANES, tq), jnp.int32))
        return jnp.sum(c, axis=0, keepdims=True)

    def search(b, prefix):
        cand = prefix | (jnp.int32(1) << (31 - b))
        cand_f = _key_to_f32(cand)
        cnt = count(lambda s: s >= cand_f)
        return jnp.where(cnt >= topk, cand, prefix)

    prefix = lax.fori_loop(0, 32, search, jnp.zeros((1, tq), jnp.int32))
    thr = jnp.where(qpos < topk, LOWEST, _key_to_f32(prefix))
    need = topk - count(lambda s: s > thr)
    n_eq = count(lambda s: s == thr)
    any_cut_tie = jnp.max(jnp.where(n_eq > need, 1, 0)) > 0
    need_f = need.astype(F32)

    acc_ref[...] = jnp.zeros(acc_ref.shape, F32)
    top_half = lax.broadcasted_iota(jnp.int32, (LANES, tq), 0) < ATT_HEAD_DIM
    for h in range(ATT_HEADS):
        pr = h // 2
        qp = qt_ref[0, pr * LANES:(pr + 1) * LANES, :]
        qh_ref[h] = jnp.where(top_half if h % 2 == 0 else ~top_half, qp, jnp.zeros_like(qp))

    def bias_plain(sc, eq_seen):
        return jnp.where(sc >= thr, 0.0, NEG), eq_seen

    def bias_ties(sc, eq_seen):
        eq = sc == thr
        before = jnp.dot(tri_ref[...], eq.astype(BF16), preferred_element_type=F32)
        sel = (sc > thr) | (eq & (before + eq_seen < need_f))
        return jnp.where(sel, 0.0, NEG), eq_seen + jnp.sum(eq.astype(F32), axis=0, keepdims=True)

    def to_bias(j, eq_seen):
        k0 = pl.multiple_of(j * tk, tk)
        bias, eq_seen = lax.cond(any_cut_tie, bias_ties, bias_plain, sc_ref[pl.ds(k0, tk), :], eq_seen)
        sc_ref[pl.ds(k0, tk), :] = bias
        return eq_seen

    lax.fori_loop(0, 2 * npair, to_bias, jnp.zeros((1, tq), F32))

    def finish(l_fin):
        rows = []
        for h in range(ATT_HEADS):
            r0 = (h % 2) * ATT_HEAD_DIM
            rows.append(acc_ref[h, r0:r0 + ATT_HEAD_DIM, :] / l_fin[h:h + 1, :])
        gate = gate_ref[...]
        o_ref[0] = jnp.concatenate(rows, axis=0).T * (gate * jax.nn.sigmoid(gate))

    if not online_max:
        def stage_logits(j, slot):
            k0 = pl.multiple_of(j * tk, tk)
            bias = sc_ref[pl.ds(k0, tk), :]
            for h in range(ATT_HEADS):
                pr = h // 2
                s_ref[slot, h] = jnp.dot(k_ref[0, pl.ds(k0, tk), pr * LANES:(pr + 1) * LANES], qh_ref[h],
                                         preferred_element_type=F32) + bias

        def consume(j, slot, l_all):
            k0 = pl.multiple_of(j * tk, tk)
            ls = []
            for h in range(ATT_HEADS):
                pr = h // 2
                p = jnp.exp2(s_ref[slot, h])
                ls.append(jnp.sum(p, axis=0, keepdims=True))
                acc_ref[h] += jnp.dot(vt_ref[0, pr * LANES:(pr + 1) * LANES, pl.ds(k0, tk)], p.astype(BF16),
                                      preferred_element_type=F32)
            return l_all + jnp.concatenate(ls, axis=0)

        def attend_bounded(jp, l_all):
            stage_logits(2 * jp + 1, 1)
            l_all = consume(2 * jp, 0, l_all)
            stage_logits(jnp.minimum(2 * jp + 2, 2 * npair - 2), 0)
            return consume(2 * jp + 1, 1, l_all)

        stage_logits(0, 0)
        finish(lax.fori_loop(0, npair, attend_bounded, jnp.zeros((ATT_HEADS, tq), F32)))
        return

    def logits_pass(j, slot):
        k0 = pl.multiple_of(j * tk, tk)
        bias = sc_ref[pl.ds(k0, tk), :]
        mx = []
        for h in range(ATT_HEADS):
            pr = h // 2
            s = jnp.dot(k_ref[0, pl.ds(k0, tk), pr * LANES:(pr + 1) * LANES], qh_ref[h],
                        preferred_element_type=F32) + bias
            s_ref[slot, h] = s
            mx.append(jnp.max(s, axis=0, keepdims=True))
        return jnp.concatenate(mx, axis=0)

    def value_pass(j, slot, m_all, l_all, mx):
        k0 = pl.multiple_of(j * tk, tk)
        m_new = jnp.maximum(m_all, mx)
        alpha = jnp.exp2(m_all - m_new)
        ls = []
        for h in range(ATT_HEADS):
            pr = h // 2
            p = jnp.exp2(s_ref[slot, h] - m_new[h:h + 1, :])
            ls.append(jnp.sum(p, axis=0, keepdims=True))
            pv = jnp.dot(vt_ref[0, pr * LANES:(pr + 1) * LANES, pl.ds(k0, tk)], p.astype(BF16),
                         preferred_element_type=F32)
            acc_ref[h] = alpha[h:h + 1, :] * acc_ref[h] + pv
        return m_new, alpha * l_all + jnp.concatenate(ls, axis=0)

    def attend(jp, carry):
        m_all, l_all, mx0 = carry
        mx1 = logits_pass(2 * jp + 1, 1)
        m_all, l_all = value_pass(2 * jp, 0, m_all, l_all, mx0)
        mx0 = logits_pass(jnp.minimum(2 * jp + 2, 2 * npair - 2), 0)
        m_all, l_all = value_pass(2 * jp + 1, 1, m_all, l_all, mx1)
        return m_all, l_all, mx0

    init = (jnp.full((ATT_HEADS, tq), NEG, F32), jnp.zeros((ATT_HEADS, tq), F32), logits_pass(0, 0))
    finish(lax.fori_loop(0, npair, attend, init)[1])


def _dsa_call(iqt, wit, qt, proj, tri, kidx, k, vt, b, l, tq, tk, online_max):
    nq = l // tq
    topk = min(TOPK_MAX, l // 4)
    kern = functools.partial(_dsa_kernel, tq=tq, tk=tk, topk=topk, online_max=online_max)
    s_stage = [pltpu.VMEM((2, ATT_HEADS, tk, tq), F32)]
    once = pl.Buffered(1)
    return pl.pallas_call(
        kern,
        grid=(b, nq),
        in_specs=[pl.BlockSpec((1, IDX_HEADS, IDX_K, tq), lambda bi, i: (bi, 0, 0, i)),
                  pl.BlockSpec((1, IDX_HEADS, tq), lambda bi, i: (bi, 0, i)),
                  pl.BlockSpec((1, ATT_WIDTH, tq), lambda bi, i: (bi, 0, i)),
                  pl.BlockSpec((tq, ATT_WIDTH), lambda bi, i: (bi * nq + i, COL_AG // ATT_WIDTH)),
                  pl.BlockSpec((tk, tk), lambda bi, i: (0, 0), pipeline_mode=once),
                  pl.BlockSpec((1, l, IDX_K), lambda bi, i: (bi, 0, 0), pipeline_mode=once),
                  pl.BlockSpec((1, l, ATT_WIDTH), lambda bi, i: (bi, 0, 0), pipeline_mode=once),
                  pl.BlockSpec((1, ATT_WIDTH, l), lambda bi, i: (bi, 0, 0), pipeline_mode=once)],
        out_specs=pl.BlockSpec((1, tq, ATT_WIDTH), lambda bi, i: (bi, i, 0)),
        out_shape=jax.ShapeDtypeStruct((b, l, ATT_WIDTH), F32),
        scratch_shapes=[pltpu.VMEM((l, tq), F32),
                        pltpu.VMEM((ATT_HEADS, LANES, tq), BF16),
                        pltpu.VMEM((ATT_HEADS, LANES, tq), F32)] + s_stage,
        compiler_params=_cparams(("arbitrary", "arbitrary")),
        name="dsa_online_max" if online_max else "dsa",
    )(iqt, wit, qt, proj, tri, kidx, k, vt)


def _mm(a, b):
    return jnp.dot(a.astype(BF16), b.astype(BF16), preferred_element_type=F32)


def _mm_nt(a, b):
    return lax.dot_general(a.astype(BF16), b.astype(BF16), (((1,), (1,)), ((), ())),
                           preferred_element_type=F32)


def _mm_exact_lhs(a01, b):
    hi = b.astype(BF16)
    lo = (b - hi.astype(F32)).astype(BF16)
    a = a01.astype(BF16)
    return jnp.dot(a, hi, preferred_element_type=F32) + jnp.dot(a, lo, preferred_element_type=F32)


def _dn_kernel(dq_ref, dk_ref, dv_ref, dz_ref, dba_ref, cw_ref, avec_ref, bvec_ref, nw_ref,
               o_ref, ext_ref, state_ref, *, rb):
    step = pl.program_id(1)
    halo = SUBLANES

    @pl.when(step == 0)
    def _():
        ext_ref[:, 0:halo, :] = jnp.zeros((3, halo, DN_WIDTH), F32)
        state_ref[...] = jnp.zeros(state_ref.shape, F32)

    def conv_silu(idx, src_ref):
        ext_ref[idx, halo:halo + rb, :] = src_ref[...]
        y = jnp.zeros((rb, DN_WIDTH), F32)
        for j in range(CONV_KERNEL):
            off = halo - (CONV_KERNEL - 1) + j
            y = y + ext_ref[idx, off:off + rb, :] * cw_ref[j:j + 1, idx * DN_WIDTH:(idx + 1) * DN_WIDTH]
        ext_ref[idx, 0:halo, :] = ext_ref[idx, rb:rb + halo, :]
        return y * jax.nn.sigmoid(y)

    def l2n(t):
        return t * lax.rsqrt(jnp.sum(t * t, axis=-1, keepdims=True) + EPS)

    qa = conv_silu(0, dq_ref)
    ka = conv_silu(1, dk_ref)
    va = conv_silu(2, dv_ref)

    dba = dba_ref[...]
    beta_all = jax.nn.sigmoid(dba)
    xg = dba + bvec_ref[...]
    softplus = jnp.maximum(xg, 0.0) + jnp.log1p(jnp.exp(-jnp.abs(xg)))
    g_all = -jnp.exp(avec_ref[...]) * softplus

    r = lax.broadcasted_iota(jnp.int32, (rb, rb), 0)
    c = lax.broadcasted_iota(jnp.int32, (rb, rb), 1)
    same_chunk = (r // CHUNK) == (c // CHUNK)
    lower = same_chunk & (r >= c)
    strict = same_chunk & (r > c)
    gc_all = _mm_exact_lhs(lower, g_all)
    gc_rows = gc_all.T

    for h in range(DN_HEADS):
        sl = slice(h * DN_HEAD_DIM, (h + 1) * DN_HEAD_DIM)
        q = l2n(qa[:, sl]) * (DN_HEAD_DIM ** -0.5)
        k = l2n(ka[:, sl])
        beta = beta_all[:, h:h + 1]
        gc = gc_all[:, DN_HEADS + h:DN_HEADS + h + 1]
        decay = jnp.exp(jnp.where(lower, gc - gc_rows[DN_HEADS + h:DN_HEADS + h + 1, :], NEG))
        k_beta = k * beta
        nmat = -jnp.where(strict, _mm_nt(k_beta, k) * decay, 0.0)
        sol = jnp.concatenate([va[:, sl] * beta, k_beta * jnp.exp(gc)], axis=-1)
        for it in range(6):
            sol = sol + _mm(nmat, sol)
            if it < 5:
                nmat = _mm(nmat, nmat)
        u, w = sol[:, 0:DN_HEAD_DIM], sol[:, DN_HEAD_DIM:]
        intra = jnp.where(lower, _mm_nt(q, k) * decay, 0.0)
        fold = intra[:, 0:LANES]
        for t in range(1, rb // LANES):
            fold = fold + intra[:, t * LANES:(t + 1) * LANES]
        fold = (fold + pltpu.roll(fold, CHUNK, 1))[:, 0:CHUNK]
        qg = q * jnp.exp(gc)
        state = state_ref[h]
        outs = []
        for ci in range(rb // CHUNK):
            cs = slice(ci * CHUNK, (ci + 1) * CHUNK)
            v_new = u[cs] - _mm(w[cs], state)
            outs.append(_mm(qg[cs], state) + _mm(fold[cs], v_new))
            g_last = gc[(ci + 1) * CHUNK - 1:(ci + 1) * CHUNK, :]
            kdec = k[cs] * jnp.exp(g_last - gc[cs])
            state = state * jnp.exp(g_last) + _mm(kdec.T, v_new)
        state_ref[h] = state
        o = jnp.concatenate(outs, axis=0)
        z = dz_ref[:, sl]
        on = o * lax.rsqrt(jnp.mean(o * o, axis=-1, keepdims=True) + EPS) * nw_ref[...]
        o_ref[:, sl] = on * (z * jax.nn.sigmoid(z))


def _dn_call(proj, conv_w, avec, bvec, norm_w, b, l, rb=256):
    nr = l // rb
    kern = functools.partial(_dn_kernel, rb=rb)

    def col(base):
        return lambda bi, i: (bi * nr + i, base // DN_WIDTH)

    return pl.pallas_call(
        kern,
        grid=(b, nr),
        in_specs=[pl.BlockSpec((rb, DN_WIDTH), col(COL_DQ)),
                  pl.BlockSpec((rb, DN_WIDTH), col(COL_DK)),
                  pl.BlockSpec((rb, DN_WIDTH), col(COL_DV)),
                  pl.BlockSpec((rb, DN_WIDTH), col(COL_DZ)),
                  pl.BlockSpec((rb, LANES), lambda bi, i: (bi * nr + i, COL_DBA // LANES)),
                  pl.BlockSpec((CONV_KERNEL, 3 * DN_WIDTH), lambda bi, i: (0, 0)),
                  pl.BlockSpec((1, LANES), lambda bi, i: (0, 0)),
                  pl.BlockSpec((1, LANES), lambda bi, i: (0, 0)),
                  pl.BlockSpec((1, DN_HEAD_DIM), lambda bi, i: (0, 0))],
        out_specs=pl.BlockSpec((rb, DN_WIDTH), lambda bi, i: (bi * nr + i, 0)),
        out_shape=jax.ShapeDtypeStruct((b * l, DN_WIDTH), F32),
        scratch_shapes=[pltpu.VMEM((3, rb + SUBLANES, DN_WIDTH), F32),
                        pltpu.VMEM((DN_HEADS, DN_HEAD_DIM, DN_HEAD_DIM), F32)],
        compiler_params=_cparams(("arbitrary", "arbitrary")),
        name="deltanet",
    )(proj, proj, proj, proj, proj, conv_w, avec, bvec, norm_w)


def _out_kernel(x_ref, oa_ref, od_ref, wa_ref, wd_ref, o_ref):
    acc = jnp.dot(oa_ref[...].astype(BF16), wa_ref[...], preferred_element_type=F32)
    acc = acc + jnp.dot(od_ref[...].astype(BF16), wd_ref[...], preferred_element_type=F32)
    o_ref[...] = x_ref[...] + acc


def _out_call(x2, oa, od, wa, wd, tm=512):
    n = x2.shape[0]
    return pl.pallas_call(
        _out_kernel,
        grid=(n // tm,),
        in_specs=[pl.BlockSpec((tm, D_MODEL), lambda i: (i, 0)),
                  pl.BlockSpec((tm, ATT_WIDTH), lambda i: (i, 0)),
                  pl.BlockSpec((tm, DN_WIDTH), lambda i: (i, 0)),
                  pl.BlockSpec((ATT_WIDTH, D_MODEL), lambda i: (0, 0)),
                  pl.BlockSpec((DN_WIDTH, D_MODEL), lambda i: (0, 0))],
        out_specs=pl.BlockSpec((tm, D_MODEL), lambda i: (i, 0)),
        out_shape=jax.ShapeDtypeStruct((n, D_MODEL), F32),
        compiler_params=_cparams(("arbitrary",)),
        name="out_proj",
    )(x2, oa, od, wa, wd)


def _layer(h, ln_w, w_in, attn_q_norm_w, attn_k_norm_w, idx_k_norm_w, idx_k_norm_b,
           dn_conv_w, dn_a_log, dn_dt_bias, dn_norm_w, w_out):
    b, l, _ = h.shape
    x2 = h.reshape(b * l, D_MODEL)

    n_ikw = IDX_HEAD_DIM + IDX_HEADS
    src_ikw = COL_IQ + IDX_HEADS * IDX_HEAD_DIM
    src_dn = src_ikw + n_ikw
    src_dba = src_dn + 4 * DN_WIDTH
    w_pad = jnp.concatenate(
        [w_in[:, :src_ikw], w_in[:, src_dn:src_dba],
         w_in[:, src_ikw:src_dn], jnp.zeros((D_MODEL, LANES - n_ikw), F32),
         w_in[:, src_dba:], jnp.zeros((D_MODEL, LANES - 2 * DN_HEADS), F32)], axis=1).astype(BF16)
    grp = jnp.arange(ATT_WIDTH) // ATT_HEAD_DIM
    gmat = (grp[:, None] == grp[None, :]).astype(BF16)
    wq_t = jnp.tile(attn_q_norm_w, ATT_HEADS)[None, :]
    wk_t = jnp.tile(attn_k_norm_w, ATT_HEADS)[None, :]
    lnw_p = jnp.pad(idx_k_norm_w, (0, LANES - IDX_HEAD_DIM))[None, :]
    lnb_p = jnp.pad(idx_k_norm_b, (0, LANES - IDX_HEAD_DIM))[None, :]
    avec = jnp.pad(dn_a_log, (DN_HEADS, LANES - 2 * DN_HEADS))[None, :]
    bvec = jnp.pad(dn_dt_bias, (DN_HEADS, LANES - 2 * DN_HEADS))[None, :]
    tq = min(256, l)
    tk = min(256, l)
    tri = (jnp.arange(tk)[None, :] < jnp.arange(tk)[:, None]).astype(BF16)

    proj = _proj_call(x2, ln_w[None, :], w_pad)
    qt, k, vt = _attn_prep_call(proj, gmat, wq_t, wk_t, b, l)
    iqt, kidx, wit = _idx_prep_call(proj, lnw_p, lnb_p, b, l)
    logit_bound = (ATT_HEAD_DIM ** 0.5 * LOG2E) * jnp.max(jnp.abs(attn_q_norm_w)) * jnp.max(jnp.abs(attn_k_norm_w))
    dsa_args = (iqt, wit, qt, proj, tri, kidx, k, vt)
    o_a = lax.cond(logit_bound * BF16_SLACK < LOGIT_SAFE,
                   lambda *a: _dsa_call(*a, b, l, tq, tk, online_max=False),
                   lambda *a: _dsa_call(*a, b, l, tq, tk, online_max=True), *dsa_args)
    o_d = _dn_call(proj, dn_conv_w, avec, bvec, dn_norm_w[None, :], b, l)
    out = _out_call(x2, o_a.reshape(b * l, ATT_WIDTH), o_d,
                    w_out[:ATT_WIDTH].astype(BF16), w_out[ATT_WIDTH:].astype(BF16))
    return out.reshape(b, l, D_MODEL)


def kernel(x, ln_w, w_in, attn_q_norm_w, attn_k_norm_w, idx_k_norm_w, idx_k_norm_b, dn_conv_w, dn_A_log,
           dn_dt_bias, dn_norm_w, w_out):
    h = x
    for layer in range(ln_w.shape[0]):
        h = _layer(h, ln_w[layer], w_in[layer], attn_q_norm_w[layer], attn_k_norm_w[layer],
                   idx_k_norm_w[layer], idx_k_norm_b[layer], dn_conv_w[layer], dn_A_log[layer],
                   dn_dt_bias[layer], dn_norm_w[layer], w_out[layer])
    return h
```

```python
import functools

import jax
import jax.numpy as jnp
from jax import lax
from jax.experimental import pallas as pl
from jax.experimental.pallas import tpu as pltpu

F32 = jnp.float32
BF16 = jnp.bfloat16

D_MODEL = 1024
ATT_HEADS = 8
ATT_HEAD_DIM = 64
ATT_WIDTH = ATT_HEADS * ATT_HEAD_DIM
IDX_HEADS = 8
IDX_HEAD_DIM = 64
TOPK_MAX = 256
DN_HEADS = 4
DN_HEAD_DIM = 128
DN_WIDTH = DN_HEADS * DN_HEAD_DIM
CONV_KERNEL = 4
CHUNK = 64
EPS = 1e-6
NEG = -1e30
LANES = 128
SUBLANES = 8
LOWEST = -3.0e38
LOG2E = 1.4426950408889634
LOGIT_SAFE = 60.0
BF16_SLACK = 1.02

COL_AQ, COL_AK, COL_AV, COL_AG = 0, 512, 1024, 1536
COL_IQ = 2048
COL_DQ, COL_DK, COL_DV, COL_DZ = 2560, 3072, 3584, 4096
COL_IKW = 4608
COL_DBA = 4736
D_PAD = 4864
N_SPLIT = 2
IDX_K = 4 * IDX_HEAD_DIM

VMEM_LIMIT = 60 * 1024 * 1024


def _cparams(sem, flags=None):
    return pltpu.CompilerParams(dimension_semantics=sem, vmem_limit_bytes=VMEM_LIMIT, flags=flags)


def _proj_kernel(x_ref, lnw_ref, w_ref, o_ref):
    xf = x_ref[...]
    ms = jnp.mean(xf * xf, axis=-1, keepdims=True)
    hn = xf * lax.rsqrt(ms + EPS) * lnw_ref[...]
    o_ref[...] = jnp.dot(hn.astype(BF16), w_ref[...], preferred_element_type=F32)


def _proj_call(x2, ln_w, w_pad, tm=512):
    n = x2.shape[0]
    tn = D_PAD // N_SPLIT
    return pl.pallas_call(
        _proj_kernel,
        grid=(N_SPLIT, n // tm),
        in_specs=[pl.BlockSpec((tm, D_MODEL), lambda j, i: (i, 0)),
                  pl.BlockSpec((1, D_MODEL), lambda j, i: (0, 0)),
                  pl.BlockSpec((D_MODEL, tn), lambda j, i: (0, j))],
        out_specs=pl.BlockSpec((tm, tn), lambda j, i: (i, j)),
        out_shape=jax.ShapeDtypeStruct((n, D_PAD), F32),
        compiler_params=_cparams(("arbitrary", "arbitrary")),
        name="proj",
    )(x2, ln_w, w_pad)


def _group_sumsq(x, g):
    sq = x * x
    hi = sq.astype(BF16)
    lo = (sq - hi.astype(F32)).astype(BF16)
    return (jnp.dot(hi, g, preferred_element_type=F32) + jnp.dot(lo, g, preferred_element_type=F32))


def _attn_prep_kernel(aq_ref, ak_ref, av_ref, g_ref, wq_ref, wk_ref, qt_ref, k_ref, vt_ref):
    g = g_ref[...]
    aq = aq_ref[...]
    ak = ak_ref[...]
    inv_d = 1.0 / ATT_HEAD_DIM
    qn = aq * lax.rsqrt(_group_sumsq(aq, g) * inv_d + EPS) * wq_ref[...]
    kn = ak * lax.rsqrt(_group_sumsq(ak, g) * inv_d + EPS) * wk_ref[...]
    qt_ref[0] = (qn * (ATT_HEAD_DIM ** -0.5 * LOG2E)).T.astype(BF16)
    k_ref[0] = kn.astype(BF16)
    vt_ref[0] = av_ref[...].T.astype(BF16)


def _attn_prep_call(proj, gmat, wq_t, wk_t, b, l, tr=512):
    nr = l // tr
    wblk = ATT_WIDTH
    return pl.pallas_call(
        _attn_prep_kernel,
        grid=(b, nr),
        in_specs=[pl.BlockSpec((tr, wblk), lambda bi, i: (bi * nr + i, COL_AQ // wblk)),
                  pl.BlockSpec((tr, wblk), lambda bi, i: (bi * nr + i, COL_AK // wblk)),
                  pl.BlockSpec((tr, wblk), lambda bi, i: (bi * nr + i, COL_AV // wblk)),
                  pl.BlockSpec((wblk, wblk), lambda bi, i: (0, 0)),
                  pl.BlockSpec((1, wblk), lambda bi, i: (0, 0)),
                  pl.BlockSpec((1, wblk), lambda bi, i: (0, 0))],
        out_specs=[pl.BlockSpec((1, wblk, tr), lambda bi, i: (bi, 0, i)),
                   pl.BlockSpec((1, tr, wblk), lambda bi, i: (bi, i, 0)),
                   pl.BlockSpec((1, wblk, tr), lambda bi, i: (bi, 0, i))],
        out_shape=[jax.ShapeDtypeStruct((b, wblk, l), BF16),
                   jax.ShapeDtypeStruct((b, l, wblk), BF16),
                   jax.ShapeDtypeStruct((b, wblk, l), BF16)],
        compiler_params=_cparams(("arbitrary", "arbitrary")),
        name="attn_prep",
    )(proj, proj, proj, gmat, wq_t, wk_t)


def _hi_lo(x):
    hi = x.astype(BF16).astype(F32)
    return hi, x - hi


def _idx_prep_kernel(iq_ref, ikw_ref, lnw_ref, lnb_ref, iqt_ref, kidx_ref, wit_ref):
    tr = iq_ref.shape[0]
    lane = lax.broadcasted_iota(jnp.int32, (tr, LANES), 1)
    low = lane < IDX_HEAD_DIM

    for j in range(IDX_HEADS // 2):
        d = iq_ref[:, j * LANES:(j + 1) * LANES]
        r = pltpu.roll(d, IDX_HEAD_DIM, 1)
        for half, dup in enumerate((jnp.where(low, d, r), jnp.where(low, r, d))):
            hi, lo = _hi_lo(dup)
            h = 2 * j + half
            iqt_ref[0, h, 0:LANES, :] = jnp.where(low, hi, lo).T.astype(BF16)
            iqt_ref[0, h, LANES:2 * LANES, :] = jnp.where(low, hi, 0.0).T.astype(BF16)

    ikw = ikw_ref[...]
    inv_d = 1.0 / IDX_HEAD_DIM
    mu = jnp.sum(jnp.where(low, ikw, 0.0), axis=-1, keepdims=True) * inv_d
    cen = jnp.where(low, ikw - mu, 0.0)
    var = jnp.sum(cen * cen, axis=-1, keepdims=True) * inv_d
    kn = jnp.where(low, cen * lax.rsqrt(var + EPS) * lnw_ref[...] + lnb_ref[...], 0.0)
    hi, lo = _hi_lo(kn)
    kidx_ref[0, :, 0:LANES] = (hi + pltpu.roll(hi, IDX_HEAD_DIM, 1)).astype(BF16)
    kidx_ref[0, :, LANES:2 * LANES] = lo.astype(BF16)

    scale = (IDX_HEADS ** -0.5) * (IDX_HEAD_DIM ** -0.5)
    wit_ref[0] = (ikw * scale).T[IDX_HEAD_DIM:IDX_HEAD_DIM + IDX_HEADS, :]


def _idx_prep_call(proj, lnw_p, lnb_p, b, l, tr=512):
    nr = l // tr
    return pl.pallas_call(
        _idx_prep_kernel,
        grid=(b, nr),
        in_specs=[pl.BlockSpec((tr, 512), lambda bi, i: (bi * nr + i, COL_IQ // 512)),
                  pl.BlockSpec((tr, LANES), lambda bi, i: (bi * nr + i, COL_IKW // LANES)),
                  pl.BlockSpec((1, LANES), lambda bi, i: (0, 0)),
                  pl.BlockSpec((1, LANES), lambda bi, i: (0, 0))],
        out_specs=[pl.BlockSpec((1, IDX_HEADS, IDX_K, tr), lambda bi, i: (bi, 0, 0, i)),
                   pl.BlockSpec((1, tr, IDX_K), lambda bi, i: (bi, i, 0)),
                   pl.BlockSpec((1, IDX_HEADS, tr), lambda bi, i: (bi, 0, i))],
        out_shape=[jax.ShapeDtypeStruct((b, IDX_HEADS, IDX_K, l), BF16),
                   jax.ShapeDtypeStruct((b, l, IDX_K), BF16),
                   jax.ShapeDtypeStruct((b, IDX_HEADS, l), F32)],
        compiler_params=_cparams(("arbitrary", "arbitrary")),
        name="idx_prep",
    )(proj, proj, lnw_p, lnb_p)


_FLIP = 0x7FFFFFFF
COARSE_BITS = 16
HALF_CELL = 1 << (31 - COARSE_BITS)


def _key_to_bits(key):
    return jnp.where(key >= 0, key, key ^ _FLIP)


def _key_to_f32(key):
    return lax.bitcast_convert_type(_key_to_bits(key), F32)


def _dsa_kernel(iqt_ref, wit_ref, qt_ref, gate_ref, tri_ref, kidx_ref, k_ref, vt_ref, o_ref,
                sc_ref, sb_ref, qh_ref, acc_ref, s_ref, *, tq, tk, topk, online_max):
    i = pl.program_id(1)
    q0 = i * tq
    nkt = (q0 + tq + tk - 1) // tk
    qpos = q0 + lax.broadcasted_iota(jnp.int32, (1, tq), 1)
    krow = lax.broadcasted_iota(jnp.int32, (tk, tq), 0)

    npair = (nkt + 1) // 2

    def score_pair(jp, carry):
        for t in range(2):
            k0 = pl.multiple_of((2 * jp + t) * tk, tk)
            kk = kidx_ref[0, pl.ds(k0, tk), :]
            tot = jnp.zeros((tk, tq), F32)
            for h in range(IDX_HEADS):
                s = jnp.dot(kk, iqt_ref[0, h], preferred_element_type=F32)
                tot = tot + jnp.maximum(s, 0.0) * wit_ref[0, h:h + 1, :]
            sc = jnp.where(k0 + krow <= qpos, tot, -jnp.inf)
            sc_ref[pl.ds(k0, tk), :] = sc
            sb_ref[pl.ds(k0, tk), :] = sc.astype(BF16)
        return carry

    lax.fori_loop(0, npair, score_pair, 0)

    def over_tiles(body, init):
        return lax.fori_loop(0, npair,
                             lambda j, c: c + body(pl.multiple_of(j * (2 * tk), 2 * tk), 2 * tk), init)

    acc_rows = 4 * SUBLANES

    def count(pred):
        def body(k0, rows):
            hit = pred(sc_ref[pl.ds(k0, rows), :]).astype(jnp.int32)
            return jnp.sum(hit.reshape(rows // acc_rows, acc_rows, tq), axis=0)
        return jnp.sum(over_tiles(body, jnp.zeros((acc_rows, tq), jnp.int32)), axis=0, keepdims=True)

    def count_coarse(cand_b):
        one, zero = jnp.ones((), BF16), jnp.zeros((), BF16)

        def body(k0, rows):
            hit = jnp.where(sb_ref[pl.ds(k0, rows), :] >= cand_b, one, zero)
            h3 = hit.reshape(rows // acc_rows, acc_rows, tq)
            part = h3[0]
            for t in range(1, rows // acc_rows):
                part = part + h3[t]
            return part.astype(F32)
        return jnp.sum(over_tiles(body, jnp.zeros((acc_rows, tq), F32)), axis=0, keepdims=True)

    low_mask = jnp.int32(-(1 << (32 - COARSE_BITS)))

    def coarse(b, prefix):
        cand = prefix ^ (jnp.int32(1) << (31 - b))
        cand_b = lax.bitcast_convert_type(_key_to_bits(cand) & low_mask, F32).astype(BF16)
        return jnp.where(count_coarse(cand_b) >= topk, cand, prefix)

    prefix = lax.fori_loop(0, COARSE_BITS, coarse, jnp.full((1, tq), -2 ** 31, jnp.int32))
    key_p = _key_to_bits(_key_to_bits(prefix) & low_mask)

    def fine(_, st):
        lo, hi = st
        mid = lo + ((hi - lo) >> 1)
        mid_f = _key_to_f32(mid)
        ok = count(lambda s: s >= mid_f) >= topk
        return jnp.where(ok, mid, lo), jnp.where(ok, hi, mid)

    span = 3 * HALF_CELL + 2
    lo, _ = lax.fori_loop(0, span.bit_length(), fine, (key_p - (HALF_CELL + 1), key_p + (2 * HALF_CELL + 1)))
    thr = jnp.where(qpos < topk, LOWEST, _key_to_f32(lo))
    need = topk - count(lambda s: s > thr)
    n_eq = count(lambda s: s == thr)
    any_cut_tie = jnp.max(jnp.where(n_eq > need, 1, 0)) > 0
    need_f = need.astype(F32)

    acc_ref[...] = jnp.zeros(acc_ref.shape, F32)
    top_half = lax.broadcasted_iota(jnp.int32, (LANES, tq), 0) < ATT_HEAD_DIM
    for h in range(ATT_HEADS):
        pr = h // 2
        qp = qt_ref[0, pr * LANES:(pr + 1) * LANES, :]
        qh_ref[h] = jnp.where(top_half if h % 2 == 0 else ~top_half, qp, jnp.zeros_like(qp))

    @pl.when(any_cut_tie)
    def _():
        def resolve(j, eq_seen):
            k0 = pl.multiple_of(j * tk, tk)
            sc = sc_ref[pl.ds(k0, tk), :]
            eq = sc == thr
            before = jnp.dot(tri_ref[...], eq.astype(BF16), preferred_element_type=F32)
            sel = (sc > thr) | (eq & (before + eq_seen < need_f))
            sc_ref[pl.ds(k0, tk), :] = jnp.where(sel, jnp.inf, -jnp.inf)
            return eq_seen + jnp.sum(eq.astype(F32), axis=0, keepdims=True)

        lax.fori_loop(0, 2 * npair, resolve, jnp.zeros((1, tq), F32))

    def mask_bias(k0):
        return jnp.where(sc_ref[pl.ds(k0, tk), :] >= thr, 0.0, NEG)

    def finish(l_fin):
        rows = []
        for h in range(ATT_HEADS):
            r0 = (h % 2) * ATT_HEAD_DIM
            rows.append(acc_ref[h, r0:r0 + ATT_HEAD_DIM, :] / l_fin[h:h + 1, :])
        gate = gate_ref[...]
        o_ref[0] = jnp.concatenate(rows, axis=0).T * (gate * jax.nn.sigmoid(gate))

    if not online_max:
        def stage_logits(j, slot):
            k0 = pl.multiple_of(j * tk, tk)
            bias = mask_bias(k0)
            for h in range(ATT_HEADS):
                pr = h // 2
                s_ref[slot, h] = jnp.dot(k_ref[0, pl.ds(k0, tk), pr * LANES:(pr + 1) * LANES], qh_ref[h],
                                         preferred_element_type=F32) + bias

        def consume(j, slot, l_all):
            k0 = pl.multiple_of(j * tk, tk)
            ls = []
            for h in range(ATT_HEADS):
                pr = h // 2
                p = jnp.exp2(s_ref[slot, h])
                ls.append(jnp.sum(p, axis=0, keepdims=True))
                acc_ref[h] += jnp.dot(vt_ref[0, pr * LANES:(pr + 1) * LANES, pl.ds(k0, tk)], p.astype(BF16),
                                      preferred_element_type=F32)
            return l_all + jnp.concatenate(ls, axis=0)

        def attend_bounded(jp, l_all):
            stage_logits(2 * jp + 1, 1)
            l_all = consume(2 * jp, 0, l_all)
            stage_logits(jnp.minimum(2 * jp + 2, 2 * npair - 2), 0)
            return consume(2 * jp + 1, 1, l_all)

        stage_logits(0, 0)
        finish(lax.fori_loop(0, npair, attend_bounded, jnp.zeros((ATT_HEADS, tq), F32)))
        return

    def logits_pass(j, slot):
        k0 = pl.multiple_of(j * tk, tk)
        bias = mask_bias(k0)
        mx = []
        for h in range(ATT_HEADS):
            pr = h // 2
            s = jnp.dot(k_ref[0, pl.ds(k0, tk), pr * LANES:(pr + 1) * LANES], qh_ref[h],
                        preferred_element_type=F32) + bias
            s_ref[slot, h] = s
            mx.append(jnp.max(s, axis=0, keepdims=True))
        return jnp.concatenate(mx, axis=0)

    def value_pass(j, slot, m_all, l_all, mx):
        k0 = pl.multiple_of(j * tk, tk)
        m_new = jnp.maximum(m_all, mx)
        alpha = jnp.exp2(m_all - m_new)
        ls = []
        for h in range(ATT_HEADS):
            pr = h // 2
            p = jnp.exp2(s_ref[slot, h] - m_new[h:h + 1, :])
            ls.append(jnp.sum(p, axis=0, keepdims=True))
            pv = jnp.dot(vt_ref[0, pr * LANES:(pr + 1) * LANES, pl.ds(k0, tk)], p.astype(BF16),
                         preferred_element_type=F32)
            acc_ref[h] = alpha[h:h + 1, :] * acc_ref[h] + pv
        return m_new, alpha * l_all + jnp.concatenate(ls, axis=0)

    def attend(jp, carry):
        m_all, l_all, mx0 = carry
        mx1 = logits_pass(2 * jp + 1, 1)
        m_all, l_all = value_pass(2 * jp, 0, m_all, l_all, mx0)
        mx0 = logits_pass(jnp.minimum(2 * jp + 2, 2 * npair - 2), 0)
        m_all, l_all = value_pass(2 * jp + 1, 1, m_all, l_all, mx1)
        return m_all, l_all, mx0

    init = (jnp.full((ATT_HEADS, tq), NEG, F32), jnp.zeros((ATT_HEADS, tq), F32), logits_pass(0, 0))
    finish(lax.fori_loop(0, npair, attend, init)[1])


def _dsa_call(iqt, wit, qt, proj, tri, kidx, k, vt, b, l, tq, tk, online_max):
    nq = l // tq
    topk = min(TOPK_MAX, l // 4)
    kern = functools.partial(_dsa_kernel, tq=tq, tk=tk, topk=topk, online_max=online_max)
    s_stage = [pltpu.VMEM((2, ATT_HEADS, tk, tq), F32)]
    once = pl.Buffered(1)
    return pl.pallas_call(
        kern,
        grid=(b, nq),
        in_specs=[pl.BlockSpec((1, IDX_HEADS, IDX_K, tq), lambda bi, i: (bi, 0, 0, i)),
                  pl.BlockSpec((1, IDX_HEADS, tq), lambda bi, i: (bi, 0, i)),
                  pl.BlockSpec((1, ATT_WIDTH, tq), lambda bi, i: (bi, 0, i)),
                  pl.BlockSpec((tq, ATT_WIDTH), lambda bi, i: (bi * nq + i, COL_AG // ATT_WIDTH)),
                  pl.BlockSpec((tk, tk), lambda bi, i: (0, 0), pipeline_mode=once),
                  pl.BlockSpec((1, l, IDX_K), lambda bi, i: (bi, 0, 0), pipeline_mode=once),
                  pl.BlockSpec((1, l, ATT_WIDTH), lambda bi, i: (bi, 0, 0), pipeline_mode=once),
                  pl.BlockSpec((1, ATT_WIDTH, l), lambda bi, i: (bi, 0, 0), pipeline_mode=once)],
        out_specs=pl.BlockSpec((1, tq, ATT_WIDTH), lambda bi, i: (bi, i, 0)),
        out_shape=jax.ShapeDtypeStruct((b, l, ATT_WIDTH), F32),
        scratch_shapes=[pltpu.VMEM((l, tq), F32),
                        pltpu.VMEM((l, tq), BF16),
                        pltpu.VMEM((ATT_HEADS, LANES, tq), BF16),
                        pltpu.VMEM((ATT_HEADS, LANES, tq), F32)] + s_stage,
        compiler_params=_cparams(("arbitrary", "arbitrary")),
        name="dsa_online_max" if online_max else "dsa",
    )(iqt, wit, qt, proj, tri, kidx, k, vt)


def _mm(a, b):
    return jnp.dot(a.astype(BF16), b.astype(BF16), preferred_element_type=F32)


def _mm_nt(a, b):
    return lax.dot_general(a.astype(BF16), b.astype(BF16), (((1,), (1,)), ((), ())),
                           preferred_element_type=F32)


def _mm_exact_lhs(a01, b):
    hi = b.astype(BF16)
    lo = (b - hi.astype(F32)).astype(BF16)
    a = a01.astype(BF16)
    return jnp.dot(a, hi, preferred_element_type=F32) + jnp.dot(a, lo, preferred_element_type=F32)


def _dn_kernel(dq_ref, dk_ref, dv_ref, dz_ref, dba_ref, cw_ref, avec_ref, bvec_ref, nw_ref,
               o_ref, ext_ref, state_ref, *, rb):
    step = pl.program_id(1)
    halo = SUBLANES

    @pl.when(step == 0)
    def _():
        ext_ref[:, 0:halo, :] = jnp.zeros((3, halo, DN_WIDTH), F32)
        state_ref[...] = jnp.zeros(state_ref.shape, F32)

    def conv_silu(idx, src_ref):
        ext_ref[idx, halo:halo + rb, :] = src_ref[...]
        y = jnp.zeros((rb, DN_WIDTH), F32)
        for j in range(CONV_KERNEL):
            off = halo - (CONV_KERNEL - 1) + j
            y = y + ext_ref[idx, off:off + rb, :] * cw_ref[j:j + 1, idx * DN_WIDTH:(idx + 1) * DN_WIDTH]
        ext_ref[idx, 0:halo, :] = ext_ref[idx, rb:rb + halo, :]
        return y * jax.nn.sigmoid(y)

    def l2n(t):
        return t * lax.rsqrt(jnp.sum(t * t, axis=-1, keepdims=True) + EPS)

    qa = conv_silu(0, dq_ref)
    ka = conv_silu(1, dk_ref)
    va = conv_silu(2, dv_ref)

    dba = dba_ref[...]
    beta_all = jax.nn.sigmoid(dba)
    xg = dba + bvec_ref[...]
    softplus = jnp.maximum(xg, 0.0) + jnp.log1p(jnp.exp(-jnp.abs(xg)))
    g_all = -jnp.exp(avec_ref[...]) * softplus

    r = lax.broadcasted_iota(jnp.int32, (rb, rb), 0)
    c = lax.broadcasted_iota(jnp.int32, (rb, rb), 1)
    same_chunk = (r // CHUNK) == (c // CHUNK)
    lower = same_chunk & (r >= c)
    strict = same_chunk & (r > c)
    gc_all = _mm_exact_lhs(lower, g_all)
    gc_rows = gc_all.T

    for h in range(DN_HEADS):
        sl = slice(h * DN_HEAD_DIM, (h + 1) * DN_HEAD_DIM)
        q = l2n(qa[:, sl]) * (DN_HEAD_DIM ** -0.5)
        k = l2n(ka[:, sl])
        beta = beta_all[:, h:h + 1]
        gc = gc_all[:, DN_HEADS + h:DN_HEADS + h + 1]
        decay = jnp.exp(jnp.where(lower, gc - gc_rows[DN_HEADS + h:DN_HEADS + h + 1, :], NEG))
        k_beta = k * beta
        nmat = -jnp.where(strict, _mm_nt(k_beta, k) * decay, 0.0)
        sol = jnp.concatenate([va[:, sl] * beta, k_beta * jnp.exp(gc)], axis=-1)
        for it in range(6):
            sol = sol + _mm(nmat, sol)
            if it < 5:
                nmat = _mm(nmat, nmat)
        u, w = sol[:, 0:DN_HEAD_DIM], sol[:, DN_HEAD_DIM:]
        intra = jnp.where(lower, _mm_nt(q, k) * decay, 0.0)
        fold = intra[:, 0:LANES]
        for t in range(1, rb // LANES):
            fold = fold + intra[:, t * LANES:(t + 1) * LANES]
        fold = (fold + pltpu.roll(fold, CHUNK, 1))[:, 0:CHUNK]
        qg = q * jnp.exp(gc)
        state = state_ref[h]
        outs = []
        for ci in range(rb // CHUNK):
            cs = slice(ci * CHUNK, (ci + 1) * CHUNK)
            v_new = u[cs] - _mm(w[cs], state)
            outs.append(_mm(qg[cs], state) + _mm(fold[cs], v_new))
            g_last = gc[(ci + 1) * CHUNK - 1:(ci + 1) * CHUNK, :]
            kdec = k[cs] * jnp.exp(g_last - gc[cs])
            state = state * jnp.exp(g_last) + _mm(kdec.T, v_new)
        state_ref[h] = state
        o = jnp.concatenate(outs, axis=0)
        z = dz_ref[:, sl]
        on = o * lax.rsqrt(jnp.mean(o * o, axis=-1, keepdims=True) + EPS) * nw_ref[...]
        o_ref[:, sl] = on * (z * jax.nn.sigmoid(z))


def _dn_call(proj, conv_w, avec, bvec, norm_w, b, l, rb=256):
    nr = l // rb
    kern = functools.partial(_dn_kernel, rb=rb)

    def col(base):
        return lambda bi, i: (bi * nr + i, base // DN_WIDTH)

    return pl.pallas_call(
        kern,
        grid=(b, nr),
        in_specs=[pl.BlockSpec((rb, DN_WIDTH), col(COL_DQ)),
                  pl.BlockSpec((rb, DN_WIDTH), col(COL_DK)),
                  pl.BlockSpec((rb, DN_WIDTH), col(COL_DV)),
                  pl.BlockSpec((rb, DN_WIDTH), col(COL_DZ)),
                  pl.BlockSpec((rb, LANES), lambda bi, i: (bi * nr + i, COL_DBA // LANES)),
                  pl.BlockSpec((CONV_KERNEL, 3 * DN_WIDTH), lambda bi, i: (0, 0)),
                  pl.BlockSpec((1, LANES), lambda bi, i: (0, 0)),
                  pl.BlockSpec((1, LANES), lambda bi, i: (0, 0)),
                  pl.BlockSpec((1, DN_HEAD_DIM), lambda bi, i: (0, 0))],
        out_specs=pl.BlockSpec((rb, DN_WIDTH), lambda bi, i: (bi * nr + i, 0)),
        out_shape=jax.ShapeDtypeStruct((b * l, DN_WIDTH), F32),
        scratch_shapes=[pltpu.VMEM((3, rb + SUBLANES, DN_WIDTH), F32),
                        pltpu.VMEM((DN_HEADS, DN_HEAD_DIM, DN_HEAD_DIM), F32)],
        compiler_params=_cparams(("arbitrary", "arbitrary")),
        name="deltanet",
    )(proj, proj, proj, proj, proj, conv_w, avec, bvec, norm_w)


def _out_kernel(x_ref, oa_ref, od_ref, wa_ref, wd_ref, o_ref):
    acc = jnp.dot(oa_ref[...].astype(BF16), wa_ref[...], preferred_element_type=F32)
    acc = acc + jnp.dot(od_ref[...].astype(BF16), wd_ref[...], preferred_element_type=F32)
    o_ref[...] = x_ref[...] + acc


def _out_call(x2, oa, od, wa, wd, tm=512):
    n = x2.shape[0]
    return pl.pallas_call(
        _out_kernel,
        grid=(n // tm,),
        in_specs=[pl.BlockSpec((tm, D_MODEL), lambda i: (i, 0)),
                  pl.BlockSpec((tm, ATT_WIDTH), lambda i: (i, 0)),
                  pl.BlockSpec((tm, DN_WIDTH), lambda i: (i, 0)),
                  pl.BlockSpec((ATT_WIDTH, D_MODEL), lambda i: (0, 0)),
                  pl.BlockSpec((DN_WIDTH, D_MODEL), lambda i: (0, 0))],
        out_specs=pl.BlockSpec((tm, D_MODEL), lambda i: (i, 0)),
        out_shape=jax.ShapeDtypeStruct((n, D_MODEL), F32),
        compiler_params=_cparams(("arbitrary",)),
        name="out_proj",
    )(x2, oa, od, wa, wd)


def _layer(h, ln_w, w_in, attn_q_norm_w, attn_k_norm_w, idx_k_norm_w, idx_k_norm_b,
           dn_conv_w, dn_a_log, dn_dt_bias, dn_norm_w, w_out):
    b, l, _ = h.shape
    x2 = h.reshape(b * l, D_MODEL)

    n_ikw = IDX_HEAD_DIM + IDX_HEADS
    src_ikw = COL_IQ + IDX_HEADS * IDX_HEAD_DIM
    src_dn = src_ikw + n_ikw
    src_dba = src_dn + 4 * DN_WIDTH
    w_pad = jnp.concatenate(
        [w_in[:, :src_ikw], w_in[:, src_dn:src_dba],
         w_in[:, src_ikw:src_dn], jnp.zeros((D_MODEL, LANES - n_ikw), F32),
         w_in[:, src_dba:], jnp.zeros((D_MODEL, LANES - 2 * DN_HEADS), F32)], axis=1).astype(BF16)
    grp = jnp.arange(ATT_WIDTH) // ATT_HEAD_DIM
    gmat = (grp[:, None] == grp[None, :]).astype(BF16)
    wq_t = jnp.tile(attn_q_norm_w, ATT_HEADS)[None, :]
    wk_t = jnp.tile(attn_k_norm_w, ATT_HEADS)[None, :]
    lnw_p = jnp.pad(idx_k_norm_w, (0, LANES - IDX_HEAD_DIM))[None, :]
    lnb_p = jnp.pad(idx_k_norm_b, (0, LANES - IDX_HEAD_DIM))[None, :]
    avec = jnp.pad(dn_a_log, (DN_HEADS, LANES - 2 * DN_HEADS))[None, :]
    bvec = jnp.pad(dn_dt_bias, (DN_HEADS, LANES - 2 * DN_HEADS))[None, :]
    tq = min(256, l)
    tk = min(256, l)
    tri = (jnp.arange(tk)[None, :] < jnp.arange(tk)[:, None]).astype(BF16)

    proj = _proj_call(x2, ln_w[None, :], w_pad)
    qt, k, vt = _attn_prep_call(proj, gmat, wq_t, wk_t, b, l)
    iqt, kidx, wit = _idx_prep_call(proj, lnw_p, lnb_p, b, l)
    logit_bound = (ATT_HEAD_DIM ** 0.5 * LOG2E) * jnp.max(jnp.abs(attn_q_norm_w)) * jnp.max(jnp.abs(attn_k_norm_w))
    dsa_args = (iqt, wit, qt, proj, tri, kidx, k, vt)
    o_a = lax.cond(logit_bound * BF16_SLACK < LOGIT_SAFE,
                   lambda *a: _dsa_call(*a, b, l, tq, tk, online_max=False),
                   lambda *a: _dsa_call(*a, b, l, tq, tk, online_max=True), *dsa_args)
    o_d = _dn_call(proj, dn_conv_w, avec, bvec, dn_norm_w[None, :], b, l)
    out = _out_call(x2, o_a.reshape(b * l, ATT_WIDTH), o_d,
                    w_out[:ATT_WIDTH].astype(BF16), w_out[ATT_WIDTH:].astype(BF16))
    return out.reshape(b, l, D_MODEL)


def kernel(x, ln_w, w_in, attn_q_norm_w, attn_k_norm_w, idx_k_norm_w, idx_k_norm_b, dn_conv_w, dn_A_log,
           dn_dt_bias, dn_norm_w, w_out):
    h = x
    for layer in range(ln_w.shape[0]):
        h = _layer(h, ln_w[layer], w_in[layer], attn_q_norm_w[layer], attn_k_norm_w[layer],
                   idx_k_norm_w[layer], idx_k_norm_b[layer], dn_conv_w[layer], dn_A_log[layer],
                   dn_dt_bias[layer], dn_norm_w[layer], w_out[layer])
    return h
```

```python
import functools

import jax
import jax.numpy as jnp
from jax import lax
from jax.experimental import pallas as pl
from jax.experimental.pallas import tpu as pltpu

F32 = jnp.float32
BF16 = jnp.bfloat16

D_MODEL = 1024
ATT_HEADS = 8
ATT_HEAD_DIM = 64
ATT_WIDTH = ATT_HEADS * ATT_HEAD_DIM
IDX_HEADS = 8
IDX_HEAD_DIM = 64
TOPK_MAX = 256
DN_HEADS = 4
DN_HEAD_DIM = 128
DN_WIDTH = DN_HEADS * DN_HEAD_DIM
CONV_KERNEL = 4
CHUNK = 64
EPS = 1e-6
NEG = -1e30
LANES = 128
SUBLANES = 8
LOWEST = -3.0e38
LOG2E = 1.4426950408889634
LOGIT_SAFE = 60.0
BF16_SLACK = 1.02

COL_AQ, COL_AK, COL_AV, COL_AG = 0, 512, 1024, 1536
COL_IQ = 2048
COL_DQ, COL_DK, COL_DV, COL_DZ = 2560, 3072, 3584, 4096
COL_IKW = 4608
COL_DBA = 4736
D_PAD = 4864
N_SPLIT = 2
IDX_K = 4 * IDX_HEAD_DIM

VMEM_LIMIT = 60 * 1024 * 1024


def _cparams(sem, flags=None):
    return pltpu.CompilerParams(dimension_semantics=sem, vmem_limit_bytes=VMEM_LIMIT, flags=flags)


def _proj_kernel(x_ref, lnw_ref, w_ref, o_ref):
    xf = x_ref[...]
    ms = jnp.mean(xf * xf, axis=-1, keepdims=True)
    hn = xf * lax.rsqrt(ms + EPS) * lnw_ref[...]
    o_ref[...] = jnp.dot(hn.astype(BF16), w_ref[...], preferred_element_type=F32)


def _proj_call(x2, ln_w, w_pad, tm=512):
    n = x2.shape[0]
    tn = D_PAD // N_SPLIT
    return pl.pallas_call(
        _proj_kernel,
        grid=(N_SPLIT, n // tm),
        in_specs=[pl.BlockSpec((tm, D_MODEL), lambda j, i: (i, 0)),
                  pl.BlockSpec((1, D_MODEL), lambda j, i: (0, 0)),
                  pl.BlockSpec((D_MODEL, tn), lambda j, i: (0, j))],
        out_specs=pl.BlockSpec((tm, tn), lambda j, i: (i, j)),
        out_shape=jax.ShapeDtypeStruct((n, D_PAD), F32),
        compiler_params=_cparams(("arbitrary", "arbitrary")),
        name="proj",
    )(x2, ln_w, w_pad)


def _group_sumsq(x, g):
    sq = x * x
    hi = sq.astype(BF16)
    lo = (sq - hi.astype(F32)).astype(BF16)
    return (jnp.dot(hi, g, preferred_element_type=F32) + jnp.dot(lo, g, preferred_element_type=F32))


def _attn_prep_kernel(aq_ref, ak_ref, av_ref, g_ref, wq_ref, wk_ref, qt_ref, k_ref, vt_ref):
    g = g_ref[...]
    aq = aq_ref[...]
    ak = ak_ref[...]
    inv_d = 1.0 / ATT_HEAD_DIM
    qn = aq * lax.rsqrt(_group_sumsq(aq, g) * inv_d + EPS) * wq_ref[...]
    kn = ak * lax.rsqrt(_group_sumsq(ak, g) * inv_d + EPS) * wk_ref[...]
    qt_ref[0] = (qn * (ATT_HEAD_DIM ** -0.5 * LOG2E)).T.astype(BF16)
    k_ref[0] = kn.astype(BF16)
    vt_ref[0] = av_ref[...].T.astype(BF16)


def _attn_prep_call(proj, gmat, wq_t, wk_t, b, l, tr=512):
    nr = l // tr
    wblk = ATT_WIDTH
    return pl.pallas_call(
        _attn_prep_kernel,
        grid=(b, nr),
        in_specs=[pl.BlockSpec((tr, wblk), lambda bi, i: (bi * nr + i, COL_AQ // wblk)),
                  pl.BlockSpec((tr, wblk), lambda bi, i: (bi * nr + i, COL_AK // wblk)),
                  pl.BlockSpec((tr, wblk), lambda bi, i: (bi * nr + i, COL_AV // wblk)),
                  pl.BlockSpec((wblk, wblk), lambda bi, i: (0, 0)),
                  pl.BlockSpec((1, wblk), lambda bi, i: (0, 0)),
                  pl.BlockSpec((1, wblk), lambda bi, i: (0, 0))],
        out_specs=[pl.BlockSpec((1, wblk, tr), lambda bi, i: (bi, 0, i)),
                   pl.BlockSpec((1, tr, wblk), lambda bi, i: (bi, i, 0)),
                   pl.BlockSpec((1, wblk, tr), lambda bi, i: (bi, 0, i))],
        out_shape=[jax.ShapeDtypeStruct((b, wblk, l), BF16),
                   jax.ShapeDtypeStruct((b, l, wblk), BF16),
                   jax.ShapeDtypeStruct((b, wblk, l), BF16)],
        compiler_params=_cparams(("arbitrary", "arbitrary")),
        name="attn_prep",
    )(proj, proj, proj, gmat, wq_t, wk_t)


def _hi_lo(x):
    hi = x.astype(BF16).astype(F32)
    return hi, x - hi


def _idx_prep_kernel(iq_ref, ikw_ref, lnw_ref, lnb_ref, iqt_ref, kidx_ref, wit_ref):
    tr = iq_ref.shape[0]
    lane = lax.broadcasted_iota(jnp.int32, (tr, LANES), 1)
    low = lane < IDX_HEAD_DIM

    for j in range(IDX_HEADS // 2):
        d = iq_ref[:, j * LANES:(j + 1) * LANES]
        r = pltpu.roll(d, IDX_HEAD_DIM, 1)
        for half, dup in enumerate((jnp.where(low, d, r), jnp.where(low, r, d))):
            hi, lo = _hi_lo(dup)
            h = 2 * j + half
            iqt_ref[0, h, 0:LANES, :] = jnp.where(low, hi, lo).T.astype(BF16)
            iqt_ref[0, h, LANES:2 * LANES, :] = jnp.where(low, hi, 0.0).T.astype(BF16)

    ikw = ikw_ref[...]
    inv_d = 1.0 / IDX_HEAD_DIM
    mu = jnp.sum(jnp.where(low, ikw, 0.0), axis=-1, keepdims=True) * inv_d
    cen = jnp.where(low, ikw - mu, 0.0)
    var = jnp.sum(cen * cen, axis=-1, keepdims=True) * inv_d
    kn = jnp.where(low, cen * lax.rsqrt(var + EPS) * lnw_ref[...] + lnb_ref[...], 0.0)
    hi, lo = _hi_lo(kn)
    kidx_ref[0, :, 0:LANES] = (hi + pltpu.roll(hi, IDX_HEAD_DIM, 1)).astype(BF16)
    kidx_ref[0, :, LANES:2 * LANES] = lo.astype(BF16)

    scale = (IDX_HEADS ** -0.5) * (IDX_HEAD_DIM ** -0.5)
    wit_ref[0] = (ikw * scale).T[IDX_HEAD_DIM:IDX_HEAD_DIM + IDX_HEADS, :]


def _idx_prep_call(proj, lnw_p, lnb_p, b, l, tr=512):
    nr = l // tr
    return pl.pallas_call(
        _idx_prep_kernel,
        grid=(b, nr),
        in_specs=[pl.BlockSpec((tr, 512), lambda bi, i: (bi * nr + i, COL_IQ // 512)),
                  pl.BlockSpec((tr, LANES), lambda bi, i: (bi * nr + i, COL_IKW // LANES)),
                  pl.BlockSpec((1, LANES), lambda bi, i: (0, 0)),
                  pl.BlockSpec((1, LANES), lambda bi, i: (0, 0))],
        out_specs=[pl.BlockSpec((1, IDX_HEADS, IDX_K, tr), lambda bi, i: (bi, 0, 0, i)),
                   pl.BlockSpec((1, tr, IDX_K), lambda bi, i: (bi, i, 0)),
                   pl.BlockSpec((1, IDX_HEADS, tr), lambda bi, i: (bi, 0, i))],
        out_shape=[jax.ShapeDtypeStruct((b, IDX_HEADS, IDX_K, l), BF16),
                   jax.ShapeDtypeStruct((b, l, IDX_K), BF16),
                   jax.ShapeDtypeStruct((b, IDX_HEADS, l), F32)],
        compiler_params=_cparams(("arbitrary", "arbitrary")),
        name="idx_prep",
    )(proj, proj, lnw_p, lnb_p)


_FLIP = 0x7FFFFFFF
COARSE_BITS = 16
HALF_CELL = 1 << (31 - COARSE_BITS)


def _key_to_bits(key):
    return jnp.where(key >= 0, key, key ^ _FLIP)


def _key_to_f32(key):
    return lax.bitcast_convert_type(_key_to_bits(key), F32)


def _dsa_kernel(iqt_ref, wit_ref, qt_ref, gate_ref, tri_ref, kidx_ref, k_ref, vt_ref, o_ref,
                sc_ref, sb_ref, qh_ref, acc_ref, s_ref, *, tq, tk, topk, online_max):
    i = pl.program_id(1)
    q0 = i * tq
    nkt = (q0 + tq + tk - 1) // tk
    qpos = q0 + lax.broadcasted_iota(jnp.int32, (1, tq), 1)
    krow = lax.broadcasted_iota(jnp.int32, (tk, tq), 0)

    npair = (nkt + 1) // 2

    def score_pair(jp, carry):
        for t in range(2):
            k0 = pl.multiple_of((2 * jp + t) * tk, tk)
            kk = kidx_ref[0, pl.ds(k0, tk), :]
            tot = jnp.zeros((tk, tq), F32)
            for h in range(IDX_HEADS):
                s = jnp.dot(kk, iqt_ref[0, h], preferred_element_type=F32)
                tot = tot + jnp.maximum(s, 0.0) * wit_ref[0, h:h + 1, :]
            sc = jnp.where(k0 + krow <= qpos, tot, -jnp.inf)
            sc_ref[pl.ds(k0, tk), :] = sc
            sb_ref[pl.ds(k0, tk), :] = sc.astype(BF16)
        return carry

    lax.fori_loop(0, npair, score_pair, 0)

    def over_tiles(body, init):
        return lax.fori_loop(0, npair,
                             lambda j, c: c + body(pl.multiple_of(j * (2 * tk), 2 * tk), 2 * tk), init)

    acc_rows = 4 * SUBLANES

    def count(pred):
        def body(k0, rows):
            hit = pred(sc_ref[pl.ds(k0, rows), :]).astype(jnp.int32)
            return jnp.sum(hit.reshape(rows // acc_rows, acc_rows, tq), axis=0)
        return jnp.sum(over_tiles(body, jnp.zeros((acc_rows, tq), jnp.int32)), axis=0, keepdims=True)

    def count_coarse(cand_b):
        one, zero = jnp.ones((), BF16), jnp.zeros((), BF16)

        def body(k0, rows):
            hit = jnp.where(sb_ref[pl.ds(k0, rows), :] >= cand_b, one, zero)
            h3 = hit.reshape(rows // acc_rows, acc_rows, tq)
            part = h3[0]
            for t in range(1, rows // acc_rows):
                part = part + h3[t]
            return part.astype(F32)
        return jnp.sum(over_tiles(body, jnp.zeros((acc_rows, tq), F32)), axis=0, keepdims=True)

    low_mask = jnp.int32(-(1 << (32 - COARSE_BITS)))

    def coarse(b, prefix):
        cand = prefix ^ (jnp.int32(1) << (31 - b))
        cand_b = lax.bitcast_convert_type(_key_to_bits(cand) & low_mask, F32).astype(BF16)
        return jnp.where(count_coarse(cand_b) >= topk, cand, prefix)

    prefix = lax.fori_loop(0, COARSE_BITS, coarse, jnp.full((1, tq), -2 ** 31, jnp.int32))
    key_p = _key_to_bits(_key_to_bits(prefix) & low_mask)

    def fine(_, st):
        lo, hi = st
        mid = lo + ((hi - lo) >> 1)
        mid_f = _key_to_f32(mid)
        ok = count(lambda s: s >= mid_f) >= topk
        return jnp.where(ok, mid, lo), jnp.where(ok, hi, mid)

    span = 3 * HALF_CELL + 2
    lo, _ = lax.fori_loop(0, span.bit_length(), fine, (key_p - (HALF_CELL + 1), key_p + (2 * HALF_CELL + 1)))
    thr = jnp.where(qpos < topk, LOWEST, _key_to_f32(lo))
    need = topk - count(lambda s: s > thr)
    n_eq = count(lambda s: s == thr)
    any_cut_tie = jnp.max(jnp.where(n_eq > need, 1, 0)) > 0
    need_f = need.astype(F32)

    acc_ref[...] = jnp.zeros(acc_ref.shape, F32)
    top_half = lax.broadcasted_iota(jnp.int32, (LANES, tq), 0) < ATT_HEAD_DIM
    for h in range(ATT_HEADS):
        pr = h // 2
        qp = qt_ref[0, pr * LANES:(pr + 1) * LANES, :]
        qh_ref[h] = jnp.where(top_half if h % 2 == 0 else ~top_half, qp, jnp.zeros_like(qp))

    @pl.when(any_cut_tie)
    def _():
        def resolve(j, eq_seen):
            k0 = pl.multiple_of(j * tk, tk)
            sc = sc_ref[pl.ds(k0, tk), :]
            eq = sc == thr
            before = jnp.dot(tri_ref[...], eq.astype(BF16), preferred_element_type=F32)
            sel = (sc > thr) | (eq & (before + eq_seen < need_f))
            sc_ref[pl.ds(k0, tk), :] = jnp.where(sel, jnp.inf, -jnp.inf)
            return eq_seen + jnp.sum(eq.astype(F32), axis=0, keepdims=True)

        lax.fori_loop(0, 2 * npair, resolve, jnp.zeros((1, tq), F32))

    def mask_bias(k0):
        return jnp.where(sc_ref[pl.ds(k0, tk), :] >= thr, 0.0, NEG)

    def finish(l_fin):
        rows = []
        for h in range(ATT_HEADS):
            r0 = (h % 2) * ATT_HEAD_DIM
            rows.append(acc_ref[h, r0:r0 + ATT_HEAD_DIM, :] / l_fin[h:h + 1, :])
        gate = gate_ref[...]
        o_ref[0] = jnp.concatenate(rows, axis=0).T * (gate * jax.nn.sigmoid(gate))

    if not online_max:
        def stage_logits(j, slot):
            k0 = pl.multiple_of(j * tk, tk)
            bias = mask_bias(k0)
            for h in range(ATT_HEADS):
                pr = h // 2
                s_ref[slot, h] = jnp.dot(k_ref[0, pl.ds(k0, tk), pr * LANES:(pr + 1) * LANES], qh_ref[h],
                                         preferred_element_type=F32) + bias

        def consume(j, slot, l_all):
            k0 = pl.multiple_of(j * tk, tk)
            ls = []
            for h in range(ATT_HEADS):
                pr = h // 2
                p = jnp.exp2(s_ref[slot, h])
                ls.append(jnp.sum(p, axis=0, keepdims=True))
                acc_ref[h] += jnp.dot(vt_ref[0, pr * LANES:(pr + 1) * LANES, pl.ds(k0, tk)], p.astype(BF16),
                                      preferred_element_type=F32)
            return l_all + jnp.concatenate(ls, axis=0)

        def attend_bounded(jp, l_all):
            stage_logits(2 * jp + 1, 1)
            l_all = consume(2 * jp, 0, l_all)
            stage_logits(jnp.minimum(2 * jp + 2, 2 * npair - 2), 0)
            return consume(2 * jp + 1, 1, l_all)

        stage_logits(0, 0)
        finish(lax.fori_loop(0, npair, attend_bounded, jnp.zeros((ATT_HEADS, tq), F32)))
        return

    def logits_pass(j, slot):
        k0 = pl.multiple_of(j * tk, tk)
        bias = mask_bias(k0)
        mx = []
        for h in range(ATT_HEADS):
            pr = h // 2
            s = jnp.dot(k_ref[0, pl.ds(k0, tk), pr * LANES:(pr + 1) * LANES], qh_ref[h],
                        preferred_element_type=F32) + bias
            s_ref[slot, h] = s
            mx.append(jnp.max(s, axis=0, keepdims=True))
        return jnp.concatenate(mx, axis=0)

    def value_pass(j, slot, m_all, l_all, mx):
        k0 = pl.multiple_of(j * tk, tk)
        m_new = jnp.maximum(m_all, mx)
        alpha = jnp.exp2(m_all - m_new)
        ls = []
        for h in range(ATT_HEADS):
            pr = h // 2
            p = jnp.exp2(s_ref[slot, h] - m_new[h:h + 1, :])
            ls.append(jnp.sum(p, axis=0, keepdims=True))
            pv = jnp.dot(vt_ref[0, pr * LANES:(pr + 1) * LANES, pl.ds(k0, tk)], p.astype(BF16),
                         preferred_element_type=F32)
            acc_ref[h] = alpha[h:h + 1, :] * acc_ref[h] + pv
        return m_new, alpha * l_all + jnp.concatenate(ls, axis=0)

    def attend(jp, carry):
        m_all, l_all, mx0 = carry
        mx1 = logits_pass(2 * jp + 1, 1)
        m_all, l_all = value_pass(2 * jp, 0, m_all, l_all, mx0)
        mx0 = logits_pass(jnp.minimum(2 * jp + 2, 2 * npair - 2), 0)
        m_all, l_all = value_pass(2 * jp + 1, 1, m_all, l_all, mx1)
        return m_all, l_all, mx0

    init = (jnp.full((ATT_HEADS, tq), NEG, F32), jnp.zeros((ATT_HEADS, tq), F32), logits_pass(0, 0))
    finish(lax.fori_loop(0, npair, attend, init)[1])


def _dsa_call(iqt, wit, qt, proj, tri, kidx, k, vt, b, l, tq, tk, online_max):
    nq = l // tq
    topk = min(TOPK_MAX, l // 4)
    kern = functools.partial(_dsa_kernel, tq=tq, tk=tk, topk=topk, online_max=online_max)
    s_stage = [pltpu.VMEM((2, ATT_HEADS, tk, tq), F32)]
    once = pl.Buffered(1)
    return pl.pallas_call(
        kern,
        grid=(b, nq),
        in_specs=[pl.BlockSpec((1, IDX_HEADS, IDX_K, tq), lambda bi, i: (bi, 0, 0, i)),
                  pl.BlockSpec((1, IDX_HEADS, tq), lambda bi, i: (bi, 0, i)),
                  pl.BlockSpec((1, ATT_WIDTH, tq), lambda bi, i: (bi, 0, i)),
                  pl.BlockSpec((tq, ATT_WIDTH), lambda bi, i: (bi * nq + i, COL_AG // ATT_WIDTH)),
                  pl.BlockSpec((tk, tk), lambda bi, i: (0, 0), pipeline_mode=once),
                  pl.BlockSpec((1, l, IDX_K), lambda bi, i: (bi, 0, 0), pipeline_mode=once),
                  pl.BlockSpec((1, l, ATT_WIDTH), lambda bi, i: (bi, 0, 0), pipeline_mode=once),
                  pl.BlockSpec((1, ATT_WIDTH, l), lambda bi, i: (bi, 0, 0), pipeline_mode=once)],
        out_specs=pl.BlockSpec((1, tq, ATT_WIDTH), lambda bi, i: (bi, i, 0)),
        out_shape=jax.ShapeDtypeStruct((b, l, ATT_WIDTH), F32),
        scratch_shapes=[pltpu.VMEM((l, tq), F32),
                        pltpu.VMEM((l, tq), BF16),
                        pltpu.VMEM((ATT_HEADS, LANES, tq), BF16),
                        pltpu.VMEM((ATT_HEADS, LANES, tq), F32)] + s_stage,
        compiler_params=_cparams(("arbitrary", "arbitrary")),
        name="dsa_online_max" if online_max else "dsa",
    )(iqt, wit, qt, proj, tri, kidx, k, vt)


def _mm(a, b):
    return jnp.dot(a.astype(BF16), b.astype(BF16), preferred_element_type=F32)


def _mm_nt(a, b):
    return lax.dot_general(a.astype(BF16), b.astype(BF16), (((1,), (1,)), ((), ())),
                           preferred_element_type=F32)


def _mm_exact_lhs(a01, b):
    hi = b.astype(BF16)
    lo = (b - hi.astype(F32)).astype(BF16)
    a = a01.astype(BF16)
    return jnp.dot(a, hi, preferred_element_type=F32) + jnp.dot(a, lo, preferred_element_type=F32)


def _dn_kernel(dq_ref, dk_ref, dv_ref, dz_ref, dba_ref, cw_ref, avec_ref, bvec_ref, nw_ref,
               o_ref, ext_ref, state_ref, *, rb):
    step = pl.program_id(1)
    halo = SUBLANES

    @pl.when(step == 0)
    def _():
        ext_ref[:, 0:halo, :] = jnp.zeros((3, halo, DN_WIDTH), F32)
        state_ref[...] = jnp.zeros(state_ref.shape, F32)

    def conv_silu(idx, src_ref):
        ext_ref[idx, halo:halo + rb, :] = src_ref[...]
        y = jnp.zeros((rb, DN_WIDTH), F32)
        for j in range(CONV_KERNEL):
            off = halo - (CONV_KERNEL - 1) + j
            y = y + ext_ref[idx, off:off + rb, :] * cw_ref[j:j + 1, idx * DN_WIDTH:(idx + 1) * DN_WIDTH]
        ext_ref[idx, 0:halo, :] = ext_ref[idx, rb:rb + halo, :]
        return y * jax.nn.sigmoid(y)

    def l2n(t):
        return t * lax.rsqrt(jnp.sum(t * t, axis=-1, keepdims=True) + EPS)

    qa = conv_silu(0, dq_ref)
    ka = conv_silu(1, dk_ref)
    va = conv_silu(2, dv_ref)

    dba = dba_ref[...]
    beta_all = jax.nn.sigmoid(dba)
    xg = dba + bvec_ref[...]
    softplus = jnp.maximum(xg, 0.0) + jnp.log1p(jnp.exp(-jnp.abs(xg)))
    g_all = -jnp.exp(avec_ref[...]) * softplus

    r = lax.broadcasted_iota(jnp.int32, (rb, rb), 0)
    c = lax.broadcasted_iota(jnp.int32, (rb, rb), 1)
    same_chunk = (r // CHUNK) == (c // CHUNK)
    lower = same_chunk & (r >= c)
    strict = same_chunk & (r > c)
    gc_all = _mm_exact_lhs(lower, g_all)
    gc_rows = gc_all.T

    heads = range(DN_HEADS)
    sls = [slice(h * DN_HEAD_DIM, (h + 1) * DN_HEAD_DIM) for h in heads]
    qs = [l2n(qa[:, sls[h]]) * (DN_HEAD_DIM ** -0.5) for h in heads]
    ks = [l2n(ka[:, sls[h]]) for h in heads]
    gcs = [gc_all[:, DN_HEADS + h:DN_HEADS + h + 1] for h in heads]
    decays = [jnp.exp(jnp.where(lower, gcs[h] - gc_rows[DN_HEADS + h:DN_HEADS + h + 1, :], NEG))
              for h in heads]
    k_betas = [ks[h] * beta_all[:, h:h + 1] for h in heads]
    nmats = [(-jnp.where(strict, _mm_nt(k_betas[h], ks[h]) * decays[h], 0.0)).astype(BF16) for h in heads]
    sols = [jnp.concatenate([va[:, sls[h]] * beta_all[:, h:h + 1], k_betas[h] * jnp.exp(gcs[h])], axis=-1)
            for h in heads]
    for it in range(6):
        sols = [sols[h] + jnp.dot(nmats[h], sols[h].astype(BF16), preferred_element_type=F32) for h in heads]
        if it < 5:
            nmats = [jnp.dot(nmats[h], nmats[h], preferred_element_type=F32).astype(BF16) for h in heads]
    folds = []
    for h in heads:
        intra = _mm_nt(qs[h], ks[h]) * decays[h]
        fold = intra[:, 0:LANES]
        for t in range(1, rb // LANES):
            fold = fold + intra[:, t * LANES:(t + 1) * LANES]
        folds.append((fold + pltpu.roll(fold, CHUNK, 1))[:, 0:CHUNK])
    qgs = [qs[h] * jnp.exp(gcs[h]) for h in heads]
    states = [state_ref[h] for h in heads]
    outs = [[] for _ in heads]
    for ci in range(rb // CHUNK):
        cs = slice(ci * CHUNK, (ci + 1) * CHUNK)
        last = slice((ci + 1) * CHUNK - 1, (ci + 1) * CHUNK)
        v_news = [sols[h][cs, 0:DN_HEAD_DIM] - _mm(sols[h][cs, DN_HEAD_DIM:], states[h]) for h in heads]
        for h in heads:
            outs[h].append(_mm(qgs[h][cs], states[h]) + _mm(folds[h][cs], v_news[h]))
        kdecs = [ks[h][cs] * jnp.exp(gcs[h][last] - gcs[h][cs]) for h in heads]
        states = [states[h] * jnp.exp(gcs[h][last]) + _mm(kdecs[h].T, v_news[h]) for h in heads]
    for h in heads:
        state_ref[h] = states[h]
        o = jnp.concatenate(outs[h], axis=0)
        z = dz_ref[:, sls[h]]
        on = o * lax.rsqrt(jnp.mean(o * o, axis=-1, keepdims=True) + EPS) * nw_ref[...]
        o_ref[:, sls[h]] = on * (z * jax.nn.sigmoid(z))


def _dn_call(proj, conv_w, avec, bvec, norm_w, b, l, rb=256):
    nr = l // rb
    kern = functools.partial(_dn_kernel, rb=rb)

    def col(base):
        return lambda bi, i: (bi * nr + i, base // DN_WIDTH)

    return pl.pallas_call(
        kern,
        grid=(b, nr),
        in_specs=[pl.BlockSpec((rb, DN_WIDTH), col(COL_DQ)),
                  pl.BlockSpec((rb, DN_WIDTH), col(COL_DK)),
                  pl.BlockSpec((rb, DN_WIDTH), col(COL_DV)),
                  pl.BlockSpec((rb, DN_WIDTH), col(COL_DZ)),
                  pl.BlockSpec((rb, LANES), lambda bi, i: (bi * nr + i, COL_DBA // LANES)),
                  pl.BlockSpec((CONV_KERNEL, 3 * DN_WIDTH), lambda bi, i: (0, 0)),
                  pl.BlockSpec((1, LANES), lambda bi, i: (0, 0)),
                  pl.BlockSpec((1, LANES), lambda bi, i: (0, 0)),
                  pl.BlockSpec((1, DN_HEAD_DIM), lambda bi, i: (0, 0))],
        out_specs=pl.BlockSpec((rb, DN_WIDTH), lambda bi, i: (bi * nr + i, 0)),
        out_shape=jax.ShapeDtypeStruct((b * l, DN_WIDTH), F32),
        scratch_shapes=[pltpu.VMEM((3, rb + SUBLANES, DN_WIDTH), F32),
                        pltpu.VMEM((DN_HEADS, DN_HEAD_DIM, DN_HEAD_DIM), F32)],
        compiler_params=_cparams(("arbitrary", "arbitrary")),
        name="deltanet",
    )(proj, proj, proj, proj, proj, conv_w, avec, bvec, norm_w)


def _out_kernel(x_ref, oa_ref, od_ref, wa_ref, wd_ref, o_ref):
    acc = jnp.dot(oa_ref[...].astype(BF16), wa_ref[...], preferred_element_type=F32)
    acc = acc + jnp.dot(od_ref[...].astype(BF16), wd_ref[...], preferred_element_type=F32)
    o_ref[...] = x_ref[...] + acc


def _out_call(x2, oa, od, wa, wd, tm=512):
    n = x2.shape[0]
    return pl.pallas_call(
        _out_kernel,
        grid=(n // tm,),
        in_specs=[pl.BlockSpec((tm, D_MODEL), lambda i: (i, 0)),
                  pl.BlockSpec((tm, ATT_WIDTH), lambda i: (i, 0)),
                  pl.BlockSpec((tm, DN_WIDTH), lambda i: (i, 0)),
                  pl.BlockSpec((ATT_WIDTH, D_MODEL), lambda i: (0, 0)),
                  pl.BlockSpec((DN_WIDTH, D_MODEL), lambda i: (0, 0))],
        out_specs=pl.BlockSpec((tm, D_MODEL), lambda i: (i, 0)),
        out_shape=jax.ShapeDtypeStruct((n, D_MODEL), F32),
        compiler_params=_cparams(("arbitrary",)),
        name="out_proj",
    )(x2, oa, od, wa, wd)


def _layer(h, ln_w, w_in, attn_q_norm_w, attn_k_norm_w, idx_k_norm_w, idx_k_norm_b,
           dn_conv_w, dn_a_log, dn_dt_bias, dn_norm_w, w_out):
    b, l, _ = h.shape
    x2 = h.reshape(b * l, D_MODEL)

    n_ikw = IDX_HEAD_DIM + IDX_HEADS
    src_ikw = COL_IQ + IDX_HEADS * IDX_HEAD_DIM
    src_dn = src_ikw + n_ikw
    src_dba = src_dn + 4 * DN_WIDTH
    w_pad = jnp.concatenate(
        [w_in[:, :src_ikw], w_in[:, src_dn:src_dba],
         w_in[:, src_ikw:src_dn], jnp.zeros((D_MODEL, LANES - n_ikw), F32),
         w_in[:, src_dba:], jnp.zeros((D_MODEL, LANES - 2 * DN_HEADS), F32)], axis=1).astype(BF16)
    grp = jnp.arange(ATT_WIDTH) // ATT_HEAD_DIM
    gmat = (grp[:, None] == grp[None, :]).astype(BF16)
    wq_t = jnp.tile(attn_q_norm_w, ATT_HEADS)[None, :]
    wk_t = jnp.tile(attn_k_norm_w, ATT_HEADS)[None, :]
    lnw_p = jnp.pad(idx_k_norm_w, (0, LANES - IDX_HEAD_DIM))[None, :]
    lnb_p = jnp.pad(idx_k_norm_b, (0, LANES - IDX_HEAD_DIM))[None, :]
    avec = jnp.pad(dn_a_log, (DN_HEADS, LANES - 2 * DN_HEADS))[None, :]
    bvec = jnp.pad(dn_dt_bias, (DN_HEADS, LANES - 2 * DN_HEADS))[None, :]
    tq = min(256, l)
    tk = min(256, l)
    tri = (jnp.arange(tk)[None, :] < jnp.arange(tk)[:, None]).astype(BF16)

    proj = _proj_call(x2, ln_w[None, :], w_pad)
    qt, k, vt = _attn_prep_call(proj, gmat, wq_t, wk_t, b, l)
    iqt, kidx, wit = _idx_prep_call(proj, lnw_p, lnb_p, b, l)
    logit_bound = (ATT_HEAD_DIM ** 0.5 * LOG2E) * jnp.max(jnp.abs(attn_q_norm_w)) * jnp.max(jnp.abs(attn_k_norm_w))
    dsa_args = (iqt, wit, qt, proj, tri, kidx, k, vt)
    o_a = lax.cond(logit_bound * BF16_SLACK < LOGIT_SAFE,
                   lambda *a: _dsa_call(*a, b, l, tq, tk, online_max=False),
                   lambda *a: _dsa_call(*a, b, l, tq, tk, online_max=True), *dsa_args)
    o_d = _dn_call(proj, dn_conv_w, avec, bvec, dn_norm_w[None, :], b, l)
    out = _out_call(x2, o_a.reshape(b * l, ATT_WIDTH), o_d,
                    w_out[:ATT_WIDTH].astype(BF16), w_out[ATT_WIDTH:].astype(BF16))
    return out.reshape(b, l, D_MODEL)


def kernel(x, ln_w, w_in, attn_q_norm_w, attn_k_norm_w, idx_k_norm_w, idx_k_norm_b, dn_conv_w, dn_A_log,
           dn_dt_bias, dn_norm_w, w_out):
    h = x
    for layer in range(ln_w.shape[0]):
        h = _layer(h, ln_w[layer], w_in[layer], attn_q_norm_w[layer], attn_k_norm_w[layer],
                   idx_k_norm_w[layer], idx_k_norm_b[layer], dn_conv_w[layer], dn_A_log[layer],
                   dn_dt_bias[layer], dn_norm_w[layer], w_out[layer])
    return h
```

```python
import functools

import jax
import jax.numpy as jnp
from jax import lax
from jax.experimental import pallas as pl
from jax.experimental.pallas import tpu as pltpu

F32 = jnp.float32
BF16 = jnp.bfloat16

D_MODEL = 1024
ATT_HEADS = 8
ATT_HEAD_DIM = 64
ATT_WIDTH = ATT_HEADS * ATT_HEAD_DIM
IDX_HEADS = 8
IDX_HEAD_DIM = 64
TOPK_MAX = 256
DN_HEADS = 4
DN_HEAD_DIM = 128
DN_WIDTH = DN_HEADS * DN_HEAD_DIM
CONV_KERNEL = 4
CHUNK = 64
EPS = 1e-6
NEG = -1e30
LANES = 128
SUBLANES = 8
LOWEST = -3.0e38
LOG2E = 1.4426950408889634
LOGIT_SAFE = 60.0
BF16_SLACK = 1.02

COL_AQ, COL_AK, COL_AV, COL_AG = 0, 512, 1024, 1536
COL_IQ = 2048
COL_DQ, COL_DK, COL_DV, COL_DZ = 2560, 3072, 3584, 4096
COL_IKW = 4608
COL_DBA = 4736
D_PAD = 4864
IDX_K = 4 * IDX_HEAD_DIM

VMEM_LIMIT = 60 * 1024 * 1024


def _cparams(sem, flags=None):
    return pltpu.CompilerParams(dimension_semantics=sem, vmem_limit_bytes=VMEM_LIMIT, flags=flags)


def _proj_kernel(x_ref, lnw_ref, w_ref, g_ref, wq_ref, wk_ref, ilnw_ref, ilnb_ref,
                 qt_ref, k_ref, vt_ref, gate_ref, iqt_ref, kidx_ref, wit_ref, dn_ref):
    xf = x_ref[...]
    ms = jnp.mean(xf * xf, axis=-1, keepdims=True)
    hn = (xf * lax.rsqrt(ms + EPS) * lnw_ref[...]).astype(BF16)
    pa = jnp.dot(hn, w_ref[:, COL_AQ:COL_IQ], preferred_element_type=F32)
    _attn_prep(pa[:, COL_AQ:COL_AK], pa[:, COL_AK:COL_AV], pa[:, COL_AV:COL_AG],
               g_ref[...], wq_ref[...], wk_ref[...], qt_ref, k_ref, vt_ref)
    gate_ref[...] = pa[:, COL_AG:COL_IQ]
    pd = jnp.dot(hn, w_ref[:, COL_DQ:D_PAD], preferred_element_type=F32)
    dn_ref[...] = pd
    iq = jnp.dot(hn, w_ref[:, COL_IQ:COL_DQ], preferred_element_type=F32)
    _idx_prep(iq, pd[:, COL_IKW - COL_DQ:COL_IKW - COL_DQ + LANES], ilnw_ref[...], ilnb_ref[...],
              iqt_ref, kidx_ref, wit_ref)


def _proj_call(x2, ln_w, w_pad, gmat, wq_t, wk_t, lnw_p, lnb_p, b, l, tr=512):
    nr = l // tr
    wblk = ATT_WIDTH
    n_dn = D_PAD - COL_DQ

    def const(shape):
        return pl.BlockSpec(shape, lambda bi, i: (0, 0))

    return pl.pallas_call(
        _proj_kernel,
        grid=(b, nr),
        in_specs=[pl.BlockSpec((tr, D_MODEL), lambda bi, i: (bi * nr + i, 0)),
                  const((1, D_MODEL)),
                  pl.BlockSpec((D_MODEL, D_PAD), lambda bi, i: (0, 0), pipeline_mode=pl.Buffered(1)),
                  const((wblk, wblk)), const((1, wblk)), const((1, wblk)),
                  const((1, LANES)), const((1, LANES))],
        out_specs=[pl.BlockSpec((1, wblk, tr), lambda bi, i: (bi, 0, i)),
                   pl.BlockSpec((1, tr, wblk), lambda bi, i: (bi, i, 0)),
                   pl.BlockSpec((1, wblk, tr), lambda bi, i: (bi, 0, i)),
                   pl.BlockSpec((tr, wblk), lambda bi, i: (bi * nr + i, 0)),
                   pl.BlockSpec((1, IDX_HEADS, IDX_K, tr), lambda bi, i: (bi, 0, 0, i)),
                   pl.BlockSpec((1, tr, IDX_K), lambda bi, i: (bi, i, 0)),
                   pl.BlockSpec((1, IDX_HEADS, tr), lambda bi, i: (bi, 0, i)),
                   pl.BlockSpec((tr, n_dn), lambda bi, i: (bi * nr + i, 0))],
        out_shape=[jax.ShapeDtypeStruct((b, wblk, l), BF16),
                   jax.ShapeDtypeStruct((b, l, wblk), BF16),
                   jax.ShapeDtypeStruct((b, wblk, l), BF16),
                   jax.ShapeDtypeStruct((b * l, wblk), F32),
                   jax.ShapeDtypeStruct((b, IDX_HEADS, IDX_K, l), BF16),
                   jax.ShapeDtypeStruct((b, l, IDX_K), BF16),
                   jax.ShapeDtypeStruct((b, IDX_HEADS, l), F32),
                   jax.ShapeDtypeStruct((b * l, n_dn), F32)],
        compiler_params=_cparams(("arbitrary", "arbitrary")),
        name="proj",
    )(x2, ln_w, w_pad, gmat, wq_t, wk_t, lnw_p, lnb_p)


def _group_sumsq(x, g):
    sq = x * x
    hi = sq.astype(BF16)
    lo = (sq - hi.astype(F32)).astype(BF16)
    return (jnp.dot(hi, g, preferred_element_type=F32) + jnp.dot(lo, g, preferred_element_type=F32))


def _attn_prep(aq, ak, av, g, wq, wk, qt_ref, k_ref, vt_ref):
    inv_d = 1.0 / ATT_HEAD_DIM
    qn = aq * lax.rsqrt(_group_sumsq(aq, g) * inv_d + EPS) * wq
    kn = ak * lax.rsqrt(_group_sumsq(ak, g) * inv_d + EPS) * wk
    qt_ref[0] = (qn * (ATT_HEAD_DIM ** -0.5 * LOG2E)).T.astype(BF16)
    k_ref[0] = kn.astype(BF16)
    vt_ref[0] = av.T.astype(BF16)


def _hi_lo(x):
    hi = x.astype(BF16).astype(F32)
    return hi, x - hi


def _idx_prep(iq, ikw, lnw, lnb, iqt_ref, kidx_ref, wit_ref):
    tr = iq.shape[0]
    lane = lax.broadcasted_iota(jnp.int32, (tr, LANES), 1)
    low = lane < IDX_HEAD_DIM

    for j in range(IDX_HEADS // 2):
        d = iq[:, j * LANES:(j + 1) * LANES]
        r = pltpu.roll(d, IDX_HEAD_DIM, 1)
        for half, dup in enumerate((jnp.where(low, d, r), jnp.where(low, r, d))):
            hi, lo = _hi_lo(dup)
            h = 2 * j + half
            iqt_ref[0, h, 0:LANES, :] = jnp.where(low, hi, lo).T.astype(BF16)
            iqt_ref[0, h, LANES:2 * LANES, :] = jnp.where(low, hi, 0.0).T.astype(BF16)

    inv_d = 1.0 / IDX_HEAD_DIM
    mu = jnp.sum(jnp.where(low, ikw, 0.0), axis=-1, keepdims=True) * inv_d
    cen = jnp.where(low, ikw - mu, 0.0)
    var = jnp.sum(cen * cen, axis=-1, keepdims=True) * inv_d
    kn = jnp.where(low, cen * lax.rsqrt(var + EPS) * lnw + lnb, 0.0)
    hi, lo = _hi_lo(kn)
    kidx_ref[0, :, 0:LANES] = (hi + pltpu.roll(hi, IDX_HEAD_DIM, 1)).astype(BF16)
    kidx_ref[0, :, LANES:2 * LANES] = lo.astype(BF16)

    scale = (IDX_HEADS ** -0.5) * (IDX_HEAD_DIM ** -0.5)
    wit_ref[0] = (ikw * scale).T[IDX_HEAD_DIM:IDX_HEAD_DIM + IDX_HEADS, :]


_FLIP = 0x7FFFFFFF
COARSE_BITS = 16
HALF_CELL = 1 << (31 - COARSE_BITS)


def _key_to_bits(key):
    return jnp.where(key >= 0, key, key ^ _FLIP)


def _key_to_f32(key):
    return lax.bitcast_convert_type(_key_to_bits(key), F32)


def _dsa_kernel(iqt_ref, wit_ref, qt_ref, gate_ref, tri_ref, kidx_ref, k_ref, vt_ref, o_ref,
                sc_ref, sb_ref, qh_ref, acc_ref, s_ref, *, tq, tk, topk, online_max):
    i = pl.program_id(1)
    q0 = i * tq
    nkt = (q0 + tq + tk - 1) // tk
    qpos = q0 + lax.broadcasted_iota(jnp.int32, (1, tq), 1)
    krow = lax.broadcasted_iota(jnp.int32, (tk, tq), 0)

    npair = (nkt + 1) // 2

    def score_pair(jp, carry):
        for t in range(2):
            k0 = pl.multiple_of((2 * jp + t) * tk, tk)
            kk = kidx_ref[0, pl.ds(k0, tk), :]
            tot = jnp.zeros((tk, tq), F32)
            for h in range(IDX_HEADS):
                s = jnp.dot(kk, iqt_ref[0, h], preferred_element_type=F32)
                tot = tot + jnp.maximum(s, 0.0) * wit_ref[0, h:h + 1, :]
            sc = jnp.where(k0 + krow <= qpos, tot, -jnp.inf)
            sc_ref[pl.ds(k0, tk), :] = sc
            sb_ref[pl.ds(k0, tk), :] = sc.astype(BF16)
        return carry

    lax.fori_loop(0, npair, score_pair, 0)

    def over_tiles(body, init):
        return lax.fori_loop(0, npair,
                             lambda j, c: c + body(pl.multiple_of(j * (2 * tk), 2 * tk), 2 * tk), init)

    acc_rows = 4 * SUBLANES

    def count(pred):
        def body(k0, rows):
            hit = pred(sc_ref[pl.ds(k0, rows), :]).astype(jnp.int32)
            return jnp.sum(hit.reshape(rows // acc_rows, acc_rows, tq), axis=0)
        return jnp.sum(over_tiles(body, jnp.zeros((acc_rows, tq), jnp.int32)), axis=0, keepdims=True)

    def count_coarse(cand_b):
        one, zero = jnp.ones((), BF16), jnp.zeros((), BF16)

        def body(k0, rows):
            hit = jnp.where(sb_ref[pl.ds(k0, rows), :] >= cand_b, one, zero)
            h3 = hit.reshape(rows // acc_rows, acc_rows, tq)
            part = h3[0]
            for t in range(1, rows // acc_rows):
                part = part + h3[t]
            return part.astype(F32)
        return jnp.sum(over_tiles(body, jnp.zeros((acc_rows, tq), F32)), axis=0, keepdims=True)

    low_mask = jnp.int32(-(1 << (32 - COARSE_BITS)))

    def coarse(b, prefix):
        cand = prefix ^ (jnp.int32(1) << (31 - b))
        cand_b = lax.bitcast_convert_type(_key_to_bits(cand) & low_mask, F32).astype(BF16)
        return jnp.where(count_coarse(cand_b) >= topk, cand, prefix)

    prefix = lax.fori_loop(0, COARSE_BITS, coarse, jnp.full((1, tq), -2 ** 31, jnp.int32))
    key_p = _key_to_bits(_key_to_bits(prefix) & low_mask)

    def fine(_, st):
        lo, hi = st
        mid = lo + ((hi - lo) >> 1)
        mid_f = _key_to_f32(mid)
        ok = count(lambda s: s >= mid_f) >= topk
        return jnp.where(ok, mid, lo), jnp.where(ok, hi, mid)

    span = 3 * HALF_CELL + 2
    lo, _ = lax.fori_loop(0, span.bit_length(), fine, (key_p - (HALF_CELL + 1), key_p + (2 * HALF_CELL + 1)))
    thr = jnp.where(qpos < topk, LOWEST, _key_to_f32(lo))
    need = topk - count(lambda s: s > thr)
    n_eq = count(lambda s: s == thr)
    any_cut_tie = jnp.max(jnp.where(n_eq > need, 1, 0)) > 0
    need_f = need.astype(F32)

    acc_ref[...] = jnp.zeros(acc_ref.shape, F32)
    top_half = lax.broadcasted_iota(jnp.int32, (LANES, tq), 0) < ATT_HEAD_DIM
    for h in range(ATT_HEADS):
        pr = h // 2
        qp = qt_ref[0, pr * LANES:(pr + 1) * LANES, :]
        qh_ref[h] = jnp.where(top_half if h % 2 == 0 else ~top_half, qp, jnp.zeros_like(qp))

    @pl.when(any_cut_tie)
    def _():
        def resolve(j, eq_seen):
            k0 = pl.multiple_of(j * tk, tk)
            sc = sc_ref[pl.ds(k0, tk), :]
            eq = sc == thr
            before = jnp.dot(tri_ref[...], eq.astype(BF16), preferred_element_type=F32)
            sel = (sc > thr) | (eq & (before + eq_seen < need_f))
            sc_ref[pl.ds(k0, tk), :] = jnp.where(sel, jnp.inf, -jnp.inf)
            return eq_seen + jnp.sum(eq.astype(F32), axis=0, keepdims=True)

        lax.fori_loop(0, 2 * npair, resolve, jnp.zeros((1, tq), F32))

    def mask_bias(k0):
        return jnp.where(sc_ref[pl.ds(k0, tk), :] >= thr, 0.0, NEG)

    def finish(l_fin):
        rows = []
        for h in range(ATT_HEADS):
            r0 = (h % 2) * ATT_HEAD_DIM
            rows.append(acc_ref[h, r0:r0 + ATT_HEAD_DIM, :] / l_fin[h:h + 1, :])
        gate = gate_ref[...]
        o_ref[0] = (jnp.concatenate(rows, axis=0).T * (gate * jax.nn.sigmoid(gate))).astype(o_ref.dtype)

    if not online_max:
        def stage_logits(j, slot):
            k0 = pl.multiple_of(j * tk, tk)
            bias = mask_bias(k0)
            for h in range(ATT_HEADS):
                pr = h // 2
                s_ref[slot, h] = jnp.dot(k_ref[0, pl.ds(k0, tk), pr * LANES:(pr + 1) * LANES], qh_ref[h],
                                         preferred_element_type=F32) + bias

        def consume(j, slot, l_all):
            k0 = pl.multiple_of(j * tk, tk)
            ls = []
            for h in range(ATT_HEADS):
                pr = h // 2
                p = jnp.exp2(s_ref[slot, h])
                ls.append(jnp.sum(p, axis=0, keepdims=True))
                acc_ref[h] += jnp.dot(vt_ref[0, pr * LANES:(pr + 1) * LANES, pl.ds(k0, tk)], p.astype(BF16),
                                      preferred_element_type=F32)
            return l_all + jnp.concatenate(ls, axis=0)

        def attend_bounded(jp, l_all):
            stage_logits(2 * jp + 1, 1)
            l_all = consume(2 * jp, 0, l_all)
            stage_logits(jnp.minimum(2 * jp + 2, 2 * npair - 2), 0)
            return consume(2 * jp + 1, 1, l_all)

        stage_logits(0, 0)
        finish(lax.fori_loop(0, npair, attend_bounded, jnp.zeros((ATT_HEADS, tq), F32)))
        return

    def logits_pass(j, slot):
        k0 = pl.multiple_of(j * tk, tk)
        bias = mask_bias(k0)
        mx = []
        for h in range(ATT_HEADS):
            pr = h // 2
            s = jnp.dot(k_ref[0, pl.ds(k0, tk), pr * LANES:(pr + 1) * LANES], qh_ref[h],
                        preferred_element_type=F32) + bias
            s_ref[slot, h] = s
            mx.append(jnp.max(s, axis=0, keepdims=True))
        return jnp.concatenate(mx, axis=0)

    def value_pass(j, slot, m_all, l_all, mx):
        k0 = pl.multiple_of(j * tk, tk)
        m_new = jnp.maximum(m_all, mx)
        alpha = jnp.exp2(m_all - m_new)
        ls = []
        for h in range(ATT_HEADS):
            pr = h // 2
            p = jnp.exp2(s_ref[slot, h] - m_new[h:h + 1, :])
            ls.append(jnp.sum(p, axis=0, keepdims=True))
            pv = jnp.dot(vt_ref[0, pr * LANES:(pr + 1) * LANES, pl.ds(k0, tk)], p.astype(BF16),
                         preferred_element_type=F32)
            acc_ref[h] = alpha[h:h + 1, :] * acc_ref[h] + pv
        return m_new, alpha * l_all + jnp.concatenate(ls, axis=0)

    def attend(jp, carry):
        m_all, l_all, mx0 = carry
        mx1 = logits_pass(2 * jp + 1, 1)
        m_all, l_all = value_pass(2 * jp, 0, m_all, l_all, mx0)
        mx0 = logits_pass(jnp.minimum(2 * jp + 2, 2 * npair - 2), 0)
        m_all, l_all = value_pass(2 * jp + 1, 1, m_all, l_all, mx1)
        return m_all, l_all, mx0

    init = (jnp.full((ATT_HEADS, tq), NEG, F32), jnp.zeros((ATT_HEADS, tq), F32), logits_pass(0, 0))
    finish(lax.fori_loop(0, npair, attend, init)[1])


def _dsa_call(iqt, wit, qt, proj, tri, kidx, k, vt, b, l, tq, tk, online_max):
    nq = l // tq
    topk = min(TOPK_MAX, l // 4)
    kern = functools.partial(_dsa_kernel, tq=tq, tk=tk, topk=topk, online_max=online_max)
    s_stage = [pltpu.VMEM((2, ATT_HEADS, tk, tq), F32)]
    once = pl.Buffered(1)
    return pl.pallas_call(
        kern,
        grid=(b, nq),
        in_specs=[pl.BlockSpec((1, IDX_HEADS, IDX_K, tq), lambda bi, i: (bi, 0, 0, i)),
                  pl.BlockSpec((1, IDX_HEADS, tq), lambda bi, i: (bi, 0, i)),
                  pl.BlockSpec((1, ATT_WIDTH, tq), lambda bi, i: (bi, 0, i)),
                  pl.BlockSpec((tq, ATT_WIDTH), lambda bi, i: (bi * nq + i, 0)),
                  pl.BlockSpec((tk, tk), lambda bi, i: (0, 0), pipeline_mode=once),
                  pl.BlockSpec((1, l, IDX_K), lambda bi, i: (bi, 0, 0), pipeline_mode=once),
                  pl.BlockSpec((1, l, ATT_WIDTH), lambda bi, i: (bi, 0, 0), pipeline_mode=once),
                  pl.BlockSpec((1, ATT_WIDTH, l), lambda bi, i: (bi, 0, 0), pipeline_mode=once)],
        out_specs=pl.BlockSpec((1, tq, ATT_WIDTH), lambda bi, i: (bi, i, 0)),
        out_shape=jax.ShapeDtypeStruct((b, l, ATT_WIDTH), BF16),
        scratch_shapes=[pltpu.VMEM((l, tq), F32),
                        pltpu.VMEM((l, tq), BF16),
                        pltpu.VMEM((ATT_HEADS, LANES, tq), BF16),
                        pltpu.VMEM((ATT_HEADS, LANES, tq), F32)] + s_stage,
        compiler_params=_cparams(("arbitrary", "arbitrary")),
        name="dsa_online_max" if online_max else "dsa",
    )(iqt, wit, qt, proj, tri, kidx, k, vt)


def _mm(a, b):
    return jnp.dot(a.astype(BF16), b.astype(BF16), preferred_element_type=F32)


def _mm_exact_lhs(a01, b):
    hi = b.astype(BF16)
    lo = (b - hi.astype(F32)).astype(BF16)
    a = a01.astype(BF16)
    return jnp.dot(a, hi, preferred_element_type=F32) + jnp.dot(a, lo, preferred_element_type=F32)


def _dn_kernel(dq_ref, dk_ref, dv_ref, dz_ref, dba_ref, cw_ref, avec_ref, bvec_ref, nw_ref,
               tril_ref, negl_ref, noteye_ref, o_ref, ext_ref, state_ref, *, rb):
    step = pl.program_id(1)
    halo = SUBLANES

    @pl.when(step == 0)
    def _():
        ext_ref[:, 0:halo, :] = jnp.zeros((3, halo, DN_WIDTH), F32)
        state_ref[...] = jnp.zeros(state_ref.shape, F32)

    def conv_silu(idx, src_ref):
        ext_ref[idx, halo:halo + rb, :] = src_ref[...]
        y = jnp.zeros((rb, DN_WIDTH), F32)
        for j in range(CONV_KERNEL):
            off = halo - (CONV_KERNEL - 1) + j
            y = y + ext_ref[idx, off:off + rb, :] * cw_ref[j:j + 1, idx * DN_WIDTH:(idx + 1) * DN_WIDTH]
        ext_ref[idx, 0:halo, :] = ext_ref[idx, rb:rb + halo, :]
        return y * jax.nn.sigmoid(y)

    def l2n(t):
        return t * lax.rsqrt(jnp.sum(t * t, axis=-1, keepdims=True) + EPS)

    qa = conv_silu(0, dq_ref)
    ka = conv_silu(1, dk_ref)
    va = conv_silu(2, dv_ref)

    dba = dba_ref[...]
    beta_all = jax.nn.sigmoid(dba)
    xg = dba + bvec_ref[...]
    softplus = jnp.maximum(xg, 0.0) + jnp.log1p(jnp.exp(-jnp.abs(xg)))
    g_all = -jnp.exp(avec_ref[...]) * softplus

    gc_all = _mm_exact_lhs(tril_ref[...], g_all)
    gc_rows = gc_all.T
    negl = negl_ref[...]
    noteye = noteye_ref[...]

    heads = range(DN_HEADS)
    sls = [slice(h * DN_HEAD_DIM, (h + 1) * DN_HEAD_DIM) for h in heads]
    qs = [l2n(qa[:, sls[h]]) * (DN_HEAD_DIM ** -0.5) for h in heads]
    ks = [l2n(ka[:, sls[h]]) for h in heads]
    gcs = [gc_all[:, DN_HEADS + h:DN_HEADS + h + 1] for h in heads]
    decays = [jnp.exp(gcs[h] - gc_rows[DN_HEADS + h:DN_HEADS + h + 1, :] + negl)
              for h in heads]
    k_betas = [ks[h] * beta_all[:, h:h + 1] for h in heads]
    kts = [ks[h].T for h in heads]
    nmats = [(_mm(k_betas[h], kts[h]) * decays[h] * noteye).astype(BF16) for h in heads]
    sols = [jnp.concatenate([va[:, sls[h]] * beta_all[:, h:h + 1], k_betas[h] * jnp.exp(gcs[h])], axis=-1)
            for h in heads]
    for it in range(6):
        sols = [sols[h] + jnp.dot(nmats[h], sols[h].astype(BF16), preferred_element_type=F32) for h in heads]
        if it < 5:
            nmats = [jnp.dot(nmats[h], nmats[h], preferred_element_type=F32).astype(BF16) for h in heads]
    folds = []
    for h in heads:
        intra = _mm(qs[h], kts[h]) * decays[h]
        fold = intra[:, 0:LANES]
        for t in range(1, rb // LANES):
            fold = fold + intra[:, t * LANES:(t + 1) * LANES]
        folds.append((fold + pltpu.roll(fold, CHUNK, 1))[:, 0:CHUNK])
    qgs = [qs[h] * jnp.exp(gcs[h]) for h in heads]
    states = [state_ref[h] for h in heads]
    outs = [[] for _ in heads]
    for ci in range(rb // CHUNK):
        cs = slice(ci * CHUNK, (ci + 1) * CHUNK)
        last = slice((ci + 1) * CHUNK - 1, (ci + 1) * CHUNK)
        v_news = [sols[h][cs, 0:DN_HEAD_DIM] - _mm(sols[h][cs, DN_HEAD_DIM:], states[h]) for h in heads]
        for h in heads:
            outs[h].append(_mm(qgs[h][cs], states[h]) + _mm(folds[h][cs], v_news[h]))
        kdecs = [ks[h][cs] * jnp.exp(gcs[h][last] - gcs[h][cs]) for h in heads]
        states = [states[h] * jnp.exp(gcs[h][last]) + _mm(kdecs[h].T, v_news[h]) for h in heads]
    for h in heads:
        state_ref[h] = states[h]
        o = jnp.concatenate(outs[h], axis=0)
        z = dz_ref[:, sls[h]]
        on = o * lax.rsqrt(jnp.mean(o * o, axis=-1, keepdims=True) + EPS) * nw_ref[...]
        o_ref[:, sls[h]] = (on * (z * jax.nn.sigmoid(z))).astype(o_ref.dtype)


def _dn_call(proj, conv_w, avec, bvec, norm_w, b, l, rb=256):
    nr = l // rb
    kern = functools.partial(_dn_kernel, rb=rb)

    def col(base):
        return lambda bi, i: (bi * nr + i, (base - COL_DQ) // DN_WIDTH)

    r = jnp.arange(rb)
    in_lower = (r[:, None] // CHUNK == r[None, :] // CHUNK) & (r[:, None] >= r[None, :])
    tril = in_lower.astype(BF16)
    negl = jnp.where(in_lower, 0.0, NEG).astype(F32)
    noteye = -(r[:, None] != r[None, :]).astype(F32)
    const = pl.BlockSpec((rb, rb), lambda bi, i: (0, 0))

    return pl.pallas_call(
        kern,
        grid=(b, nr),
        in_specs=[pl.BlockSpec((rb, DN_WIDTH), col(COL_DQ)),
                  pl.BlockSpec((rb, DN_WIDTH), col(COL_DK)),
                  pl.BlockSpec((rb, DN_WIDTH), col(COL_DV)),
                  pl.BlockSpec((rb, DN_WIDTH), col(COL_DZ)),
                  pl.BlockSpec((rb, LANES), lambda bi, i: (bi * nr + i, (COL_DBA - COL_DQ) // LANES)),
                  pl.BlockSpec((CONV_KERNEL, 3 * DN_WIDTH), lambda bi, i: (0, 0)),
                  pl.BlockSpec((1, LANES), lambda bi, i: (0, 0)),
                  pl.BlockSpec((1, LANES), lambda bi, i: (0, 0)),
                  pl.BlockSpec((1, DN_HEAD_DIM), lambda bi, i: (0, 0)),
                  const, const, const],
        out_specs=pl.BlockSpec((rb, DN_WIDTH), lambda bi, i: (bi * nr + i, 0)),
        out_shape=jax.ShapeDtypeStruct((b * l, DN_WIDTH), BF16),
        scratch_shapes=[pltpu.VMEM((3, rb + SUBLANES, DN_WIDTH), F32),
                        pltpu.VMEM((DN_HEADS, DN_HEAD_DIM, DN_HEAD_DIM), F32)],
        compiler_params=_cparams(("arbitrary", "arbitrary")),
        name="deltanet",
    )(proj, proj, proj, proj, proj, conv_w, avec, bvec, norm_w, tril, negl, noteye)


def _out_kernel(x_ref, oa_ref, od_ref, wa_ref, wd_ref, o_ref):
    acc = jnp.dot(oa_ref[...], wa_ref[...], preferred_element_type=F32)
    acc = acc + jnp.dot(od_ref[...], wd_ref[...], preferred_element_type=F32)
    o_ref[...] = x_ref[...] + acc


def _out_call(x2, oa, od, wa, wd, tm=512):
    n = x2.shape[0]
    return pl.pallas_call(
        _out_kernel,
        grid=(n // tm,),
        in_specs=[pl.BlockSpec((tm, D_MODEL), lambda i: (i, 0)),
                  pl.BlockSpec((tm, ATT_WIDTH), lambda i: (i, 0)),
                  pl.BlockSpec((tm, DN_WIDTH), lambda i: (i, 0)),
                  pl.BlockSpec((ATT_WIDTH, D_MODEL), lambda i: (0, 0)),
                  pl.BlockSpec((DN_WIDTH, D_MODEL), lambda i: (0, 0))],
        out_specs=pl.BlockSpec((tm, D_MODEL), lambda i: (i, 0)),
        out_shape=jax.ShapeDtypeStruct((n, D_MODEL), F32),
        compiler_params=_cparams(("arbitrary",)),
        name="out_proj",
    )(x2, oa, od, wa, wd)


def _layer(h, ln_w, w_in, attn_q_norm_w, attn_k_norm_w, idx_k_norm_w, idx_k_norm_b,
           dn_conv_w, dn_a_log, dn_dt_bias, dn_norm_w, w_out):
    b, l, _ = h.shape
    x2 = h.reshape(b * l, D_MODEL)

    n_ikw = IDX_HEAD_DIM + IDX_HEADS
    src_ikw = COL_IQ + IDX_HEADS * IDX_HEAD_DIM
    src_dn = src_ikw + n_ikw
    src_dba = src_dn + 4 * DN_WIDTH
    w_pad = jnp.concatenate(
        [w_in[:, :src_ikw], w_in[:, src_dn:src_dba],
         w_in[:, src_ikw:src_dn], jnp.zeros((D_MODEL, LANES - n_ikw), F32),
         w_in[:, src_dba:], jnp.zeros((D_MODEL, LANES - 2 * DN_HEADS), F32)], axis=1).astype(BF16)
    grp = jnp.arange(ATT_WIDTH) // ATT_HEAD_DIM
    gmat = (grp[:, None] == grp[None, :]).astype(BF16)
    wq_t = jnp.tile(attn_q_norm_w, ATT_HEADS)[None, :]
    wk_t = jnp.tile(attn_k_norm_w, ATT_HEADS)[None, :]
    lnw_p = jnp.pad(idx_k_norm_w, (0, LANES - IDX_HEAD_DIM))[None, :]
    lnb_p = jnp.pad(idx_k_norm_b, (0, LANES - IDX_HEAD_DIM))[None, :]
    avec = jnp.pad(dn_a_log, (DN_HEADS, LANES - 2 * DN_HEADS))[None, :]
    bvec = jnp.pad(dn_dt_bias, (DN_HEADS, LANES - 2 * DN_HEADS))[None, :]
    tq = min(256, l)
    tk = min(256, l)
    tri = (jnp.arange(tk)[None, :] < jnp.arange(tk)[:, None]).astype(BF16)

    qt, k, vt, gate, iqt, kidx, wit, dn_in = _proj_call(x2, ln_w[None, :], w_pad, gmat, wq_t, wk_t,
                                                        lnw_p, lnb_p, b, l)
    logit_bound = (ATT_HEAD_DIM ** 0.5 * LOG2E) * jnp.max(jnp.abs(attn_q_norm_w)) * jnp.max(jnp.abs(attn_k_norm_w))
    dsa_args = (iqt, wit, qt, gate, tri, kidx, k, vt)
    o_a = lax.cond(logit_bound * BF16_SLACK < LOGIT_SAFE,
                   lambda *a: _dsa_call(*a, b, l, tq, tk, online_max=False),
                   lambda *a: _dsa_call(*a, b, l, tq, tk, online_max=True), *dsa_args)
    o_d = _dn_call(dn_in, dn_conv_w, avec, bvec, dn_norm_w[None, :], b, l)
    out = _out_call(x2, o_a.reshape(b * l, ATT_WIDTH), o_d,
                    w_out[:ATT_WIDTH].astype(BF16), w_out[ATT_WIDTH:].astype(BF16))
    return out.reshape(b, l, D_MODEL)


def kernel(x, ln_w, w_in, attn_q_norm_w, attn_k_norm_w, idx_k_norm_w, idx_k_norm_b, dn_conv_w, dn_A_log,
           dn_dt_bias, dn_norm_w, w_out):
    h = x
    for layer in range(ln_w.shape[0]):
        h = _layer(h, ln_w[layer], w_in[layer], attn_q_norm_w[layer], attn_k_norm_w[layer],
                   idx_k_norm_w[layer], idx_k_norm_b[layer], dn_conv_w[layer], dn_A_log[layer],
                   dn_dt_bias[layer], dn_norm_w[layer], w_out[layer])
    return h
```

```python
import functools

import jax
import jax.numpy as jnp
from jax import lax
from jax.experimental import pallas as pl
from jax.experimental.pallas import tpu as pltpu

F32 = jnp.float32
BF16 = jnp.bfloat16

D_MODEL = 1024
ATT_HEADS = 8
ATT_HEAD_DIM = 64
ATT_WIDTH = ATT_HEADS * ATT_HEAD_DIM
IDX_HEADS = 8
IDX_HEAD_DIM = 64
TOPK_MAX = 256
DN_HEADS = 4
DN_HEAD_DIM = 128
DN_WIDTH = DN_HEADS * DN_HEAD_DIM
CONV_KERNEL = 4
CHUNK = 64
EPS = 1e-6
NEG = -1e30
LANES = 128
SUBLANES = 8
LOWEST = -3.0e38
LOG2E = 1.4426950408889634
LOGIT_SAFE = 60.0
BF16_SLACK = 1.02

COL_AQ, COL_AK, COL_AV, COL_AG = 0, 512, 1024, 1536
COL_IQ = 2048
COL_DQ, COL_DK, COL_DV, COL_DZ = 2560, 3072, 3584, 4096
COL_IKW = 4608
COL_DBA = 4736
D_PAD = 4864
IDX_K = 4 * IDX_HEAD_DIM

VMEM_LIMIT = 60 * 1024 * 1024


def _cparams(sem, flags=None):
    return pltpu.CompilerParams(dimension_semantics=sem, vmem_limit_bytes=VMEM_LIMIT, flags=flags)


def _proj_kernel(x_ref, lnw_ref, w_ref, g_ref, wq_ref, wk_ref, ilnw_ref, ilnb_ref,
                 qt_ref, k_ref, vt_ref, gate_ref, iqt_ref, kidx_ref, wit_ref, dn_ref):
    xf = x_ref[...]
    ms = jnp.mean(xf * xf, axis=-1, keepdims=True)
    hn = (xf * lax.rsqrt(ms + EPS) * lnw_ref[...]).astype(BF16)
    pa = jnp.dot(hn, w_ref[:, COL_AQ:COL_IQ], preferred_element_type=F32)
    _attn_prep(pa[:, COL_AQ:COL_AK], pa[:, COL_AK:COL_AV], pa[:, COL_AV:COL_AG],
               g_ref[...], wq_ref[...], wk_ref[...], qt_ref, k_ref, vt_ref)
    gate_ref[...] = pa[:, COL_AG:COL_IQ]
    pd = jnp.dot(hn, w_ref[:, COL_DQ:D_PAD], preferred_element_type=F32)
    dn_ref[...] = pd
    iq = jnp.dot(hn, w_ref[:, COL_IQ:COL_DQ], preferred_element_type=F32)
    _idx_prep(iq, pd[:, COL_IKW - COL_DQ:COL_IKW - COL_DQ + LANES], ilnw_ref[...], ilnb_ref[...],
              iqt_ref, kidx_ref, wit_ref)


def _proj_call(x2, ln_w, w_pad, gmat, wq_t, wk_t, lnw_p, lnb_p, b, l, tr=512):
    nr = l // tr
    wblk = ATT_WIDTH
    n_dn = D_PAD - COL_DQ

    def const(shape):
        return pl.BlockSpec(shape, lambda bi, i: (0, 0))

    return pl.pallas_call(
        _proj_kernel,
        grid=(b, nr),
        in_specs=[pl.BlockSpec((tr, D_MODEL), lambda bi, i: (bi * nr + i, 0)),
                  const((1, D_MODEL)),
                  pl.BlockSpec((D_MODEL, D_PAD), lambda bi, i: (0, 0), pipeline_mode=pl.Buffered(1)),
                  const((wblk, wblk)), const((1, wblk)), const((1, wblk)),
                  const((1, LANES)), const((1, LANES))],
        out_specs=[pl.BlockSpec((1, wblk, tr), lambda bi, i: (bi, 0, i)),
                   pl.BlockSpec((1, tr, wblk), lambda bi, i: (bi, i, 0)),
                   pl.BlockSpec((1, wblk, tr), lambda bi, i: (bi, 0, i)),
                   pl.BlockSpec((tr, wblk), lambda bi, i: (bi * nr + i, 0)),
                   pl.BlockSpec((1, IDX_HEADS, IDX_K, tr), lambda bi, i: (bi, 0, 0, i)),
                   pl.BlockSpec((1, tr, IDX_K), lambda bi, i: (bi, i, 0)),
                   pl.BlockSpec((1, IDX_HEADS, tr), lambda bi, i: (bi, 0, i)),
                   pl.BlockSpec((tr, n_dn), lambda bi, i: (bi * nr + i, 0))],
        out_shape=[jax.ShapeDtypeStruct((b, wblk, l), BF16),
                   jax.ShapeDtypeStruct((b, l, wblk), BF16),
                   jax.ShapeDtypeStruct((b, wblk, l), BF16),
                   jax.ShapeDtypeStruct((b * l, wblk), F32),
                   jax.ShapeDtypeStruct((b, IDX_HEADS, IDX_K, l), BF16),
                   jax.ShapeDtypeStruct((b, l, IDX_K), BF16),
                   jax.ShapeDtypeStruct((b, IDX_HEADS, l), F32),
                   jax.ShapeDtypeStruct((b * l, n_dn), F32)],
        compiler_params=_cparams(("arbitrary", "arbitrary")),
        name="proj",
    )(x2, ln_w, w_pad, gmat, wq_t, wk_t, lnw_p, lnb_p)


def _group_sumsq(x, g):
    sq = x * x
    hi = sq.astype(BF16)
    lo = (sq - hi.astype(F32)).astype(BF16)
    return (jnp.dot(hi, g, preferred_element_type=F32) + jnp.dot(lo, g, preferred_element_type=F32))


def _attn_prep(aq, ak, av, g, wq, wk, qt_ref, k_ref, vt_ref):
    inv_d = 1.0 / ATT_HEAD_DIM
    qn = aq * lax.rsqrt(_group_sumsq(aq, g) * inv_d + EPS) * wq
    kn = ak * lax.rsqrt(_group_sumsq(ak, g) * inv_d + EPS) * wk
    qt_ref[0] = (qn * (ATT_HEAD_DIM ** -0.5 * LOG2E)).T.astype(BF16)
    k_ref[0] = kn.astype(BF16)
    vt_ref[0] = av.T.astype(BF16)


def _hi_lo(x):
    hi = x.astype(BF16).astype(F32)
    return hi, x - hi


def _idx_prep(iq, ikw, lnw, lnb, iqt_ref, kidx_ref, wit_ref):
    tr = iq.shape[0]
    lane = lax.broadcasted_iota(jnp.int32, (tr, LANES), 1)
    low = lane < IDX_HEAD_DIM

    for j in range(IDX_HEADS // 2):
        d = iq[:, j * LANES:(j + 1) * LANES]
        r = pltpu.roll(d, IDX_HEAD_DIM, 1)
        for half, dup in enumerate((jnp.where(low, d, r), jnp.where(low, r, d))):
            hi, lo = _hi_lo(dup)
            h = 2 * j + half
            iqt_ref[0, h, 0:LANES, :] = jnp.where(low, hi, lo).T.astype(BF16)
            iqt_ref[0, h, LANES:2 * LANES, :] = jnp.where(low, hi, 0.0).T.astype(BF16)

    inv_d = 1.0 / IDX_HEAD_DIM
    mu = jnp.sum(jnp.where(low, ikw, 0.0), axis=-1, keepdims=True) * inv_d
    cen = jnp.where(low, ikw - mu, 0.0)
    var = jnp.sum(cen * cen, axis=-1, keepdims=True) * inv_d
    kn = jnp.where(low, cen * lax.rsqrt(var + EPS) * lnw + lnb, 0.0)
    hi, lo = _hi_lo(kn)
    kidx_ref[0, :, 0:LANES] = (hi + pltpu.roll(hi, IDX_HEAD_DIM, 1)).astype(BF16)
    kidx_ref[0, :, LANES:2 * LANES] = lo.astype(BF16)

    scale = (IDX_HEADS ** -0.5) * (IDX_HEAD_DIM ** -0.5)
    wit_ref[0] = (ikw * scale).T[IDX_HEAD_DIM:IDX_HEAD_DIM + IDX_HEADS, :]


_FLIP = 0x7FFFFFFF
COARSE_BITS = 16
HALF_CELL = 1 << (31 - COARSE_BITS)
FINE_TAIL = 4


def _key_to_bits(key):
    return jnp.where(key >= 0, key, key ^ _FLIP)


def _key_to_f32(key):
    return lax.bitcast_convert_type(_key_to_bits(key), F32)


def _dsa_kernel(iqt_ref, wit_ref, qt_ref, gate_ref, tri_ref, kidx_ref, k_ref, vt_ref, o_ref,
                sc_ref, sb_ref, qh_ref, acc_ref, s_ref, *, tq, tk, topk, online_max):
    i = pl.program_id(1)
    q0 = i * tq
    nkt = (q0 + tq + tk - 1) // tk
    qpos = q0 + lax.broadcasted_iota(jnp.int32, (1, tq), 1)
    krow = lax.broadcasted_iota(jnp.int32, (tk, tq), 0)

    npair = (nkt + 1) // 2

    def score_pair(jp, carry):
        for t in range(2):
            k0 = pl.multiple_of((2 * jp + t) * tk, tk)
            kk = kidx_ref[0, pl.ds(k0, tk), :]
            tot = jnp.zeros((tk, tq), F32)
            for h in range(IDX_HEADS):
                s = jnp.dot(kk, iqt_ref[0, h], preferred_element_type=F32)
                tot = tot + jnp.maximum(s, 0.0) * wit_ref[0, h:h + 1, :]
            sc = jnp.where(k0 + krow <= qpos, tot, -jnp.inf)
            sc_ref[pl.ds(k0, tk), :] = sc
            sb_ref[pl.ds(k0, tk), :] = sc.astype(BF16)
        return carry

    lax.fori_loop(0, npair, score_pair, 0)

    def over_tiles(body, init):
        return lax.fori_loop(0, npair,
                             lambda j, c: c + body(pl.multiple_of(j * (2 * tk), 2 * tk), 2 * tk), init)

    acc_rows = 4 * SUBLANES

    def count(pred):
        def body(k0, rows):
            hit = pred(sc_ref[pl.ds(k0, rows), :]).astype(jnp.int32)
            return jnp.sum(hit.reshape(rows // acc_rows, acc_rows, tq), axis=0)
        return jnp.sum(over_tiles(body, jnp.zeros((acc_rows, tq), jnp.int32)), axis=0, keepdims=True)

    def count_coarse(cand_b):
        one, zero = jnp.ones((), BF16), jnp.zeros((), BF16)

        def body(k0, rows):
            hit = jnp.where(sb_ref[pl.ds(k0, rows), :] >= cand_b, one, zero)
            h3 = hit.reshape(rows // acc_rows, acc_rows, tq)
            part = h3[0]
            for t in range(1, rows // acc_rows):
                part = part + h3[t]
            return part.astype(F32)
        return jnp.sum(over_tiles(body, jnp.zeros((acc_rows, tq), F32)), axis=0, keepdims=True)

    low_mask = jnp.int32(-(1 << (32 - COARSE_BITS)))

    def coarse(b, prefix):
        cand = prefix ^ (jnp.int32(1) << (31 - b))
        cand_b = lax.bitcast_convert_type(_key_to_bits(cand) & low_mask, F32).astype(BF16)
        return jnp.where(count_coarse(cand_b) >= topk, cand, prefix)

    prefix = lax.fori_loop(0, COARSE_BITS, coarse, jnp.full((1, tq), -2 ** 31, jnp.int32))
    key_p = _key_to_bits(_key_to_bits(prefix) & low_mask)

    def fine(_, st):
        lo, hi, at_lo = st
        mid = lo + ((hi - lo) >> 1)
        mid_f = _key_to_f32(mid)
        cnt = count(lambda s: s >= mid_f)
        ok = cnt >= topk
        return jnp.where(ok, mid, lo), jnp.where(ok, hi, mid), jnp.where(ok, cnt, at_lo)

    few_keys = qpos < topk
    n_fine = (3 * HALF_CELL + 2).bit_length()
    st = lax.fori_loop(0, n_fine - FINE_TAIL, fine, (key_p - (HALF_CELL + 1), key_p + (2 * HALF_CELL + 1),
                                                      jnp.full((1, tq), -1, jnp.int32)))
    all_settled = jnp.min(jnp.where((st[2] == topk) | few_keys, 1, 0)) > 0

    def finish_search(st):
        lo = lax.fori_loop(0, FINE_TAIL, fine, st)[0]
        thr = jnp.where(few_keys, LOWEST, _key_to_f32(lo))
        need = topk - count(lambda s: s > thr)
        n_eq = count(lambda s: s == thr)
        return thr, need.astype(F32), jnp.max(jnp.where(n_eq > need, 1, 0)) > 0

    def settled_search(st):
        return jnp.where(few_keys, LOWEST, _key_to_f32(st[0])), jnp.zeros((1, tq), F32), jnp.zeros((), jnp.bool_)

    thr, need_f, any_cut_tie = lax.cond(all_settled, settled_search, finish_search, st)

    acc_ref[...] = jnp.zeros(acc_ref.shape, F32)
    top_half = lax.broadcasted_iota(jnp.int32, (LANES, tq), 0) < ATT_HEAD_DIM
    for h in range(ATT_HEADS):
        pr = h // 2
        qp = qt_ref[0, pr * LANES:(pr + 1) * LANES, :]
        qh_ref[h] = jnp.where(top_half if h % 2 == 0 else ~top_half, qp, jnp.zeros_like(qp))

    @pl.when(any_cut_tie)
    def _():
        def resolve(jp, eq_seen):
            for t in range(2):
                k0 = pl.multiple_of((2 * jp + t) * tk, tk)
                sc = sc_ref[pl.ds(k0, tk), :]
                eq = sc == thr
                before = jnp.dot(tri_ref[...], eq.astype(BF16), preferred_element_type=F32)
                sel = (sc > thr) | (eq & (before + eq_seen < need_f))
                sc_ref[pl.ds(k0, tk), :] = jnp.where(sel, jnp.inf, -jnp.inf)
                eq_seen = eq_seen + jnp.sum(eq.astype(F32), axis=0, keepdims=True)
            return eq_seen

        lax.fori_loop(0, npair, resolve, jnp.zeros((1, tq), F32))

    def mask_bias(k0):
        return jnp.where(sc_ref[pl.ds(k0, tk), :] >= thr, 0.0, NEG)

    def finish(l_fin):
        rows = []
        for h in range(ATT_HEADS):
            r0 = (h % 2) * ATT_HEAD_DIM
            rows.append(acc_ref[h, r0:r0 + ATT_HEAD_DIM, :] / l_fin[h:h + 1, :])
        gate = gate_ref[...]
        o_ref[0] = (jnp.concatenate(rows, axis=0).T * (gate * jax.nn.sigmoid(gate))).astype(o_ref.dtype)

    if not online_max:
        def stage_logits(j, slot):
            k0 = pl.multiple_of(j * tk, tk)
            bias = mask_bias(k0)
            for h in range(ATT_HEADS):
                pr = h // 2
                s_ref[slot, h] = jnp.dot(k_ref[0, pl.ds(k0, tk), pr * LANES:(pr + 1) * LANES], qh_ref[h],
                                         preferred_element_type=F32) + bias

        def consume(j, slot, l_all):
            k0 = pl.multiple_of(j * tk, tk)
            ls = []
            for h in range(ATT_HEADS):
                pr = h // 2
                p = jnp.exp2(s_ref[slot, h])
                ls.append(jnp.sum(p, axis=0, keepdims=True))
                acc_ref[h] += jnp.dot(vt_ref[0, pr * LANES:(pr + 1) * LANES, pl.ds(k0, tk)], p.astype(BF16),
                                      preferred_element_type=F32)
            return l_all + jnp.concatenate(ls, axis=0)

        def attend_bounded(jp, l_all):
            stage_logits(2 * jp + 1, 1)
            l_all = consume(2 * jp, 0, l_all)
            stage_logits(jnp.minimum(2 * jp + 2, 2 * npair - 2), 0)
            return consume(2 * jp + 1, 1, l_all)

        stage_logits(0, 0)
        finish(lax.fori_loop(0, npair, attend_bounded, jnp.zeros((ATT_HEADS, tq), F32)))
        return

    def logits_pass(j, slot):
        k0 = pl.multiple_of(j * tk, tk)
        bias = mask_bias(k0)
        mx = []
        for h in range(ATT_HEADS):
            pr = h // 2
            s = jnp.dot(k_ref[0, pl.ds(k0, tk), pr * LANES:(pr + 1) * LANES], qh_ref[h],
                        preferred_element_type=F32) + bias
            s_ref[slot, h] = s
            mx.append(jnp.max(s, axis=0, keepdims=True))
        return jnp.concatenate(mx, axis=0)

    def value_pass(j, slot, m_all, l_all, mx):
        k0 = pl.multiple_of(j * tk, tk)
        m_new = jnp.maximum(m_all, mx)
        alpha = jnp.exp2(m_all - m_new)
        ls = []
        for h in range(ATT_HEADS):
            pr = h // 2
            p = jnp.exp2(s_ref[slot, h] - m_new[h:h + 1, :])
            ls.append(jnp.sum(p, axis=0, keepdims=True))
            pv = jnp.dot(vt_ref[0, pr * LANES:(pr + 1) * LANES, pl.ds(k0, tk)], p.astype(BF16),
                         preferred_element_type=F32)
            acc_ref[h] = alpha[h:h + 1, :] * acc_ref[h] + pv
        return m_new, alpha * l_all + jnp.concatenate(ls, axis=0)

    def attend(jp, carry):
        m_all, l_all, mx0 = carry
        mx1 = logits_pass(2 * jp + 1, 1)
        m_all, l_all = value_pass(2 * jp, 0, m_all, l_all, mx0)
        mx0 = logits_pass(jnp.minimum(2 * jp + 2, 2 * npair - 2), 0)
        m_all, l_all = value_pass(2 * jp + 1, 1, m_all, l_all, mx1)
        return m_all, l_all, mx0

    init = (jnp.full((ATT_HEADS, tq), NEG, F32), jnp.zeros((ATT_HEADS, tq), F32), logits_pass(0, 0))
    finish(lax.fori_loop(0, npair, attend, init)[1])


def _dsa_call(iqt, wit, qt, proj, tri, kidx, k, vt, b, l, tq, tk, online_max):
    nq = l // tq
    topk = min(TOPK_MAX, l // 4)
    kern = functools.partial(_dsa_kernel, tq=tq, tk=tk, topk=topk, online_max=online_max)
    s_stage = [pltpu.VMEM((2, ATT_HEADS, tk, tq), F32)]
    once = pl.Buffered(1)
    return pl.pallas_call(
        kern,
        grid=(b, nq),
        in_specs=[pl.BlockSpec((1, IDX_HEADS, IDX_K, tq), lambda bi, i: (bi, 0, 0, i)),
                  pl.BlockSpec((1, IDX_HEADS, tq), lambda bi, i: (bi, 0, i)),
                  pl.BlockSpec((1, ATT_WIDTH, tq), lambda bi, i: (bi, 0, i)),
                  pl.BlockSpec((tq, ATT_WIDTH), lambda bi, i: (bi * nq + i, 0)),
                  pl.BlockSpec((tk, tk), lambda bi, i: (0, 0), pipeline_mode=once),
                  pl.BlockSpec((1, l, IDX_K), lambda bi, i: (bi, 0, 0), pipeline_mode=once),
                  pl.BlockSpec((1, l, ATT_WIDTH), lambda bi, i: (bi, 0, 0), pipeline_mode=once),
                  pl.BlockSpec((1, ATT_WIDTH, l), lambda bi, i: (bi, 0, 0), pipeline_mode=once)],
        out_specs=pl.BlockSpec((1, tq, ATT_WIDTH), lambda bi, i: (bi, i, 0)),
        out_shape=jax.ShapeDtypeStruct((b, l, ATT_WIDTH), BF16),
        scratch_shapes=[pltpu.VMEM((l, tq), F32),
                        pltpu.VMEM((l, tq), BF16),
                        pltpu.VMEM((ATT_HEADS, LANES, tq), BF16),
                        pltpu.VMEM((ATT_HEADS, LANES, tq), F32)] + s_stage,
        compiler_params=_cparams(("arbitrary", "arbitrary")),
        name="dsa_online_max" if online_max else "dsa",
    )(iqt, wit, qt, proj, tri, kidx, k, vt)


def _mm(a, b):
    return jnp.dot(a.astype(BF16), b.astype(BF16), preferred_element_type=F32)


def _mm_exact_lhs(a01, b):
    hi = b.astype(BF16)
    lo = (b - hi.astype(F32)).astype(BF16)
    a = a01.astype(BF16)
    return jnp.dot(a, hi, preferred_element_type=F32) + jnp.dot(a, lo, preferred_element_type=F32)


def _dn_kernel(dq_ref, dk_ref, dv_ref, dz_ref, dba_ref, cw_ref, avec_ref, bvec_ref, nw_ref,
               tril_ref, negl_ref, noteye_ref, o_ref, ext_ref, state_ref, *, rb):
    step = pl.program_id(1)
    halo = SUBLANES

    @pl.when(step == 0)
    def _():
        ext_ref[:, 0:halo, :] = jnp.zeros((3, halo, DN_WIDTH), F32)
        state_ref[...] = jnp.zeros(state_ref.shape, F32)

    def conv_silu(idx, src_ref):
        ext_ref[idx, halo:halo + rb, :] = src_ref[...]
        y = jnp.zeros((rb, DN_WIDTH), F32)
        for j in range(CONV_KERNEL):
            off = halo - (CONV_KERNEL - 1) + j
            y = y + ext_ref[idx, off:off + rb, :] * cw_ref[j:j + 1, idx * DN_WIDTH:(idx + 1) * DN_WIDTH]
        ext_ref[idx, 0:halo, :] = ext_ref[idx, rb:rb + halo, :]
        return y * jax.nn.sigmoid(y)

    def l2n(t):
        return t * lax.rsqrt(jnp.sum(t * t, axis=-1, keepdims=True) + EPS)

    qa = conv_silu(0, dq_ref)
    ka = conv_silu(1, dk_ref)
    va = conv_silu(2, dv_ref)

    dba = dba_ref[...]
    beta_all = jax.nn.sigmoid(dba)
    xg = dba + bvec_ref[...]
    softplus = jnp.maximum(xg, 0.0) + jnp.log1p(jnp.exp(-jnp.abs(xg)))
    g_all = -jnp.exp(avec_ref[...]) * softplus

    gc_all = _mm_exact_lhs(tril_ref[...], g_all)
    gc_rows = gc_all.T
    negl = negl_ref[...]
    noteye = noteye_ref[...]

    heads = range(DN_HEADS)
    sls = [slice(h * DN_HEAD_DIM, (h + 1) * DN_HEAD_DIM) for h in heads]
    qs = [l2n(qa[:, sls[h]]) * (DN_HEAD_DIM ** -0.5) for h in heads]
    ks = [l2n(ka[:, sls[h]]) for h in heads]
    gcs = [gc_all[:, DN_HEADS + h:DN_HEADS + h + 1] for h in heads]
    decays = [jnp.exp(gcs[h] - gc_rows[DN_HEADS + h:DN_HEADS + h + 1, :] + negl)
              for h in heads]
    k_betas = [ks[h] * beta_all[:, h:h + 1] for h in heads]
    kts = [ks[h].T for h in heads]
    nmats = [(_mm(k_betas[h], kts[h]) * decays[h] * noteye).astype(BF16) for h in heads]
    sols = [jnp.concatenate([va[:, sls[h]] * beta_all[:, h:h + 1], k_betas[h] * jnp.exp(gcs[h])], axis=-1)
            for h in heads]
    for it in range(6):
        sols = [sols[h] + jnp.dot(nmats[h], sols[h].astype(BF16), preferred_element_type=F32) for h in heads]
        if it < 5:
            nmats = [jnp.dot(nmats[h], nmats[h], preferred_element_type=F32).astype(BF16) for h in heads]
    folds = []
    for h in heads:
        intra = _mm(qs[h], kts[h]) * decays[h]
        fold = intra[:, 0:LANES]
        for t in range(1, rb // LANES):
            fold = fold + intra[:, t * LANES:(t + 1) * LANES]
        folds.append((fold + pltpu.roll(fold, CHUNK, 1))[:, 0:CHUNK])
    qgs = [qs[h] * jnp.exp(gcs[h]) for h in heads]
    states = [state_ref[h] for h in heads]
    outs = [[] for _ in heads]
    for ci in range(rb // CHUNK):
        cs = slice(ci * CHUNK, (ci + 1) * CHUNK)
        last = slice((ci + 1) * CHUNK - 1, (ci + 1) * CHUNK)
        v_news = [sols[h][cs, 0:DN_HEAD_DIM] - _mm(sols[h][cs, DN_HEAD_DIM:], states[h]) for h in heads]
        for h in heads:
            outs[h].append(_mm(qgs[h][cs], states[h]) + _mm(folds[h][cs], v_news[h]))
        kdecs = [ks[h][cs] * jnp.exp(gcs[h][last] - gcs[h][cs]) for h in heads]
        states = [states[h] * jnp.exp(gcs[h][last]) + _mm(kdecs[h].T, v_news[h]) for h in heads]
    for h in heads:
        state_ref[h] = states[h]
        o = jnp.concatenate(outs[h], axis=0)
        z = dz_ref[:, sls[h]]
        on = o * lax.rsqrt(jnp.mean(o * o, axis=-1, keepdims=True) + EPS) * nw_ref[...]
        o_ref[:, sls[h]] = (on * (z * jax.nn.sigmoid(z))).astype(o_ref.dtype)


def _dn_call(proj, conv_w, avec, bvec, norm_w, b, l, rb=256):
    nr = l // rb
    kern = functools.partial(_dn_kernel, rb=rb)

    def col(base):
        return lambda bi, i: (bi * nr + i, (base - COL_DQ) // DN_WIDTH)

    r = jnp.arange(rb)
    in_lower = (r[:, None] // CHUNK == r[None, :] // CHUNK) & (r[:, None] >= r[None, :])
    tril = in_lower.astype(BF16)
    negl = jnp.where(in_lower, 0.0, NEG).astype(F32)
    noteye = -(r[:, None] != r[None, :]).astype(F32)
    const = pl.BlockSpec((rb, rb), lambda bi, i: (0, 0))

    return pl.pallas_call(
        kern,
        grid=(b, nr),
        in_specs=[pl.BlockSpec((rb, DN_WIDTH), col(COL_DQ)),
                  pl.BlockSpec((rb, DN_WIDTH), col(COL_DK)),
                  pl.BlockSpec((rb, DN_WIDTH), col(COL_DV)),
                  pl.BlockSpec((rb, DN_WIDTH), col(COL_DZ)),
                  pl.BlockSpec((rb, LANES), lambda bi, i: (bi * nr + i, (COL_DBA - COL_DQ) // LANES)),
                  pl.BlockSpec((CONV_KERNEL, 3 * DN_WIDTH), lambda bi, i: (0, 0)),
                  pl.BlockSpec((1, LANES), lambda bi, i: (0, 0)),
                  pl.BlockSpec((1, LANES), lambda bi, i: (0, 0)),
                  pl.BlockSpec((1, DN_HEAD_DIM), lambda bi, i: (0, 0)),
                  const, const, const],
        out_specs=pl.BlockSpec((rb, DN_WIDTH), lambda bi, i: (bi * nr + i, 0)),
        out_shape=jax.ShapeDtypeStruct((b * l, DN_WIDTH), BF16),
        scratch_shapes=[pltpu.VMEM((3, rb + SUBLANES, DN_WIDTH), F32),
                        pltpu.VMEM((DN_HEADS, DN_HEAD_DIM, DN_HEAD_DIM), F32)],
        compiler_params=_cparams(("arbitrary", "arbitrary")),
        name="deltanet",
    )(proj, proj, proj, proj, proj, conv_w, avec, bvec, norm_w, tril, negl, noteye)


def _out_kernel(x_ref, oa_ref, od_ref, wa_ref, wd_ref, o_ref):
    acc = jnp.dot(oa_ref[...], wa_ref[...], preferred_element_type=F32)
    acc = acc + jnp.dot(od_ref[...], wd_ref[...], preferred_element_type=F32)
    o_ref[...] = x_ref[...] + acc


def _out_call(x2, oa, od, wa, wd, tm=512):
    n = x2.shape[0]
    return pl.pallas_call(
        _out_kernel,
        grid=(n // tm,),
        in_specs=[pl.BlockSpec((tm, D_MODEL), lambda i: (i, 0)),
                  pl.BlockSpec((tm, ATT_WIDTH), lambda i: (i, 0)),
                  pl.BlockSpec((tm, DN_WIDTH), lambda i: (i, 0)),
                  pl.BlockSpec((ATT_WIDTH, D_MODEL), lambda i: (0, 0)),
                  pl.BlockSpec((DN_WIDTH, D_MODEL), lambda i: (0, 0))],
        out_specs=pl.BlockSpec((tm, D_MODEL), lambda i: (i, 0)),
        out_shape=jax.ShapeDtypeStruct((n, D_MODEL), F32),
        compiler_params=_cparams(("arbitrary",)),
        name="out_proj",
    )(x2, oa, od, wa, wd)


def _layer(h, ln_w, w_in, attn_q_norm_w, attn_k_norm_w, idx_k_norm_w, idx_k_norm_b,
           dn_conv_w, dn_a_log, dn_dt_bias, dn_norm_w, w_out):
    b, l, _ = h.shape
    x2 = h.reshape(b * l, D_MODEL)

    n_ikw = IDX_HEAD_DIM + IDX_HEADS
    src_ikw = COL_IQ + IDX_HEADS * IDX_HEAD_DIM
    src_dn = src_ikw + n_ikw
    src_dba = src_dn + 4 * DN_WIDTH
    w_pad = jnp.concatenate(
        [w_in[:, :src_ikw], w_in[:, src_dn:src_dba],
         w_in[:, src_ikw:src_dn], jnp.zeros((D_MODEL, LANES - n_ikw), F32),
         w_in[:, src_dba:], jnp.zeros((D_MODEL, LANES - 2 * DN_HEADS), F32)], axis=1).astype(BF16)
    grp = jnp.arange(ATT_WIDTH) // ATT_HEAD_DIM
    gmat = (grp[:, None] == grp[None, :]).astype(BF16)
    wq_t = jnp.tile(attn_q_norm_w, ATT_HEADS)[None, :]
    wk_t = jnp.tile(attn_k_norm_w, ATT_HEADS)[None, :]
    lnw_p = jnp.pad(idx_k_norm_w, (0, LANES - IDX_HEAD_DIM))[None, :]
    lnb_p = jnp.pad(idx_k_norm_b, (0, LANES - IDX_HEAD_DIM))[None, :]
    avec = jnp.pad(dn_a_log, (DN_HEADS, LANES - 2 * DN_HEADS))[None, :]
    bvec = jnp.pad(dn_dt_bias, (DN_HEADS, LANES - 2 * DN_HEADS))[None, :]
    tq = min(256, l)
    tk = min(256, l)
    tri = (jnp.arange(tk)[None, :] < jnp.arange(tk)[:, None]).astype(BF16)

    qt, k, vt, gate, iqt, kidx, wit, dn_in = _proj_call(x2, ln_w[None, :], w_pad, gmat, wq_t, wk_t,
                                                        lnw_p, lnb_p, b, l)
    logit_bound = (ATT_HEAD_DIM ** 0.5 * LOG2E) * jnp.max(jnp.abs(attn_q_norm_w)) * jnp.max(jnp.abs(attn_k_norm_w))
    dsa_args = (iqt, wit, qt, gate, tri, kidx, k, vt)
    o_a = lax.cond(logit_bound * BF16_SLACK < LOGIT_SAFE,
                   lambda *a: _dsa_call(*a, b, l, tq, tk, online_max=False),
                   lambda *a: _dsa_call(*a, b, l, tq, tk, online_max=True), *dsa_args)
    o_d = _dn_call(dn_in, dn_conv_w, avec, bvec, dn_norm_w[None, :], b, l)
    out = _out_call(x2, o_a.reshape(b * l, ATT_WIDTH), o_d,
                    w_out[:ATT_WIDTH].astype(BF16), w_out[ATT_WIDTH:].astype(BF16))
    return out.reshape(b, l, D_MODEL)


def kernel(x, ln_w, w_in, attn_q_norm_w, attn_k_norm_w, idx_k_norm_w, idx_k_norm_b, dn_conv_w, dn_A_log,
           dn_dt_bias, dn_norm_w, w_out):
    h = x
    for layer in range(ln_w.shape[0]):
        h = _layer(h, ln_w[layer], w_in[layer], attn_q_norm_w[layer], attn_k_norm_w[layer],
                   idx_k_norm_w[layer], idx_k_norm_b[layer], dn_conv_w[layer], dn_A_log[layer],
                   dn_dt_bias[layer], dn_norm_w[layer], w_out[layer])
    return h
```

```python
import functools

import jax
import jax.numpy as jnp
from jax import lax
from jax.experimental import pallas as pl
from jax.experimental.pallas import tpu as pltpu

F32 = jnp.float32
BF16 = jnp.bfloat16

D_MODEL = 1024
ATT_HEADS = 8
ATT_HEAD_DIM = 64
ATT_WIDTH = ATT_HEADS * ATT_HEAD_DIM
IDX_HEADS = 8
IDX_HEAD_DIM = 64
TOPK_MAX = 256
DN_HEADS = 4
DN_HEAD_DIM = 128
DN_WIDTH = DN_HEADS * DN_HEAD_DIM
CONV_KERNEL = 4
CHUNK = 64
EPS = 1e-6
NEG = -1e30
LANES = 128
SUBLANES = 8
LOWEST = -3.0e38
LOG2E = 1.4426950408889634
LOGIT_SAFE = 60.0
BF16_SLACK = 1.02

COL_AQ, COL_AK, COL_AV, COL_AG = 0, 512, 1024, 1536
COL_IQ = 2048
COL_DQ, COL_DK, COL_DV, COL_DZ = 2560, 3072, 3584, 4096
COL_IKW = 4608
COL_DBA = 4736
D_PAD = 4864
IDX_K = 4 * IDX_HEAD_DIM

VMEM_LIMIT = 60 * 1024 * 1024


def _cparams(sem, flags=None):
    return pltpu.CompilerParams(dimension_semantics=sem, vmem_limit_bytes=VMEM_LIMIT, flags=flags)


def _proj_kernel(x_ref, lnw_ref, w_ref, g_ref, wq_ref, wk_ref, ilnw_ref, ilnb_ref,
                 qt_ref, k_ref, vt_ref, gate_ref, iqt_ref, kidx_ref, wit_ref, dn_ref):
    xf = x_ref[...]
    ms = jnp.mean(xf * xf, axis=-1, keepdims=True)
    hn = (xf * lax.rsqrt(ms + EPS) * lnw_ref[...]).astype(BF16)
    pa = jnp.dot(hn, w_ref[:, COL_AQ:COL_IQ], preferred_element_type=F32)
    _attn_prep(pa[:, COL_AQ:COL_AK], pa[:, COL_AK:COL_AV], pa[:, COL_AV:COL_AG],
               g_ref[...], wq_ref[...], wk_ref[...], qt_ref, k_ref, vt_ref)
    gate_ref[...] = pa[:, COL_AG:COL_IQ]
    pd = jnp.dot(hn, w_ref[:, COL_DQ:D_PAD], preferred_element_type=F32)
    dn_ref[...] = pd
    iq = jnp.dot(hn, w_ref[:, COL_IQ:COL_DQ], preferred_element_type=F32)
    _idx_prep(iq, pd[:, COL_IKW - COL_DQ:COL_IKW - COL_DQ + LANES], ilnw_ref[...], ilnb_ref[...],
              iqt_ref, kidx_ref, wit_ref)


def _proj_call(x2, ln_w, w_pad, gmat, wq_t, wk_t, lnw_p, lnb_p, b, l, tr=512):
    nr = l // tr
    wblk = ATT_WIDTH
    n_dn = D_PAD - COL_DQ

    def const(shape):
        return pl.BlockSpec(shape, lambda bi, i: (0, 0))

    return pl.pallas_call(
        _proj_kernel,
        grid=(b, nr),
        in_specs=[pl.BlockSpec((tr, D_MODEL), lambda bi, i: (bi * nr + i, 0)),
                  const((1, D_MODEL)),
                  pl.BlockSpec((D_MODEL, D_PAD), lambda bi, i: (0, 0), pipeline_mode=pl.Buffered(1)),
                  const((wblk, wblk)), const((1, wblk)), const((1, wblk)),
                  const((1, LANES)), const((1, LANES))],
        out_specs=[pl.BlockSpec((1, wblk, tr), lambda bi, i: (bi, 0, i)),
                   pl.BlockSpec((1, tr, wblk), lambda bi, i: (bi, i, 0)),
                   pl.BlockSpec((1, wblk, tr), lambda bi, i: (bi, 0, i)),
                   pl.BlockSpec((tr, wblk), lambda bi, i: (bi * nr + i, 0)),
                   pl.BlockSpec((1, IDX_HEADS, IDX_K, tr), lambda bi, i: (bi, 0, 0, i)),
                   pl.BlockSpec((1, tr, IDX_K), lambda bi, i: (bi, i, 0)),
                   pl.BlockSpec((1, IDX_HEADS, tr), lambda bi, i: (bi, 0, i)),
                   pl.BlockSpec((tr, n_dn), lambda bi, i: (bi * nr + i, 0))],
        out_shape=[jax.ShapeDtypeStruct((b, wblk, l), BF16),
                   jax.ShapeDtypeStruct((b, l, wblk), BF16),
                   jax.ShapeDtypeStruct((b, wblk, l), BF16),
                   jax.ShapeDtypeStruct((b * l, wblk), F32),
                   jax.ShapeDtypeStruct((b, IDX_HEADS, IDX_K, l), BF16),
                   jax.ShapeDtypeStruct((b, l, IDX_K), BF16),
                   jax.ShapeDtypeStruct((b, IDX_HEADS, l), F32),
                   jax.ShapeDtypeStruct((b * l, n_dn), F32)],
        compiler_params=_cparams(("arbitrary", "arbitrary")),
        name="proj",
    )(x2, ln_w, w_pad, gmat, wq_t, wk_t, lnw_p, lnb_p)


def _group_sumsq(x, g):
    sq = x * x
    hi = sq.astype(BF16)
    lo = (sq - hi.astype(F32)).astype(BF16)
    return (jnp.dot(hi, g, preferred_element_type=F32) + jnp.dot(lo, g, preferred_element_type=F32))


def _attn_prep(aq, ak, av, g, wq, wk, qt_ref, k_ref, vt_ref):
    inv_d = 1.0 / ATT_HEAD_DIM
    qn = aq * lax.rsqrt(_group_sumsq(aq, g) * inv_d + EPS) * wq
    kn = ak * lax.rsqrt(_group_sumsq(ak, g) * inv_d + EPS) * wk
    qt_ref[0] = (qn * (ATT_HEAD_DIM ** -0.5 * LOG2E)).T.astype(BF16)
    k_ref[0] = kn.astype(BF16)
    vt_ref[0] = av.T.astype(BF16)


def _hi_lo(x):
    hi = x.astype(BF16).astype(F32)
    return hi, x - hi


def _idx_prep(iq, ikw, lnw, lnb, iqt_ref, kidx_ref, wit_ref):
    tr = iq.shape[0]
    lane = lax.broadcasted_iota(jnp.int32, (tr, LANES), 1)
    low = lane < IDX_HEAD_DIM

    for j in range(IDX_HEADS // 2):
        d = iq[:, j * LANES:(j + 1) * LANES]
        r = pltpu.roll(d, IDX_HEAD_DIM, 1)
        for half, dup in enumerate((jnp.where(low, d, r), jnp.where(low, r, d))):
            hi, lo = _hi_lo(dup)
            h = 2 * j + half
            iqt_ref[0, h, 0:LANES, :] = jnp.where(low, hi, lo).T.astype(BF16)
            iqt_ref[0, h, LANES:2 * LANES, :] = jnp.where(low, hi, 0.0).T.astype(BF16)

    inv_d = 1.0 / IDX_HEAD_DIM
    mu = jnp.sum(jnp.where(low, ikw, 0.0), axis=-1, keepdims=True) * inv_d
    cen = jnp.where(low, ikw - mu, 0.0)
    var = jnp.sum(cen * cen, axis=-1, keepdims=True) * inv_d
    kn = jnp.where(low, cen * lax.rsqrt(var + EPS) * lnw + lnb, 0.0)
    hi, lo = _hi_lo(kn)
    kidx_ref[0, :, 0:LANES] = (hi + pltpu.roll(hi, IDX_HEAD_DIM, 1)).astype(BF16)
    kidx_ref[0, :, LANES:2 * LANES] = lo.astype(BF16)

    scale = (IDX_HEADS ** -0.5) * (IDX_HEAD_DIM ** -0.5)
    wit_ref[0] = (ikw * scale).T[IDX_HEAD_DIM:IDX_HEAD_DIM + IDX_HEADS, :]


_FLIP = 0x7FFFFFFF
COARSE_BITS = 16
HALF_CELL = 1 << (31 - COARSE_BITS)
FINE_TAIL = 4


def _key_to_bits(key):
    return jnp.where(key >= 0, key, key ^ _FLIP)


def _key_to_f32(key):
    return lax.bitcast_convert_type(_key_to_bits(key), F32)


def _dsa_kernel(iqt_ref, wit_ref, qt_ref, gate_ref, tri_ref, kidx_ref, k_ref, vt_ref, o_ref,
                sc_ref, sb_ref, qh_ref, acc_ref, s_ref, te_ref, cb_ref, tt_ref, *, tq, tk, topk, online_max):
    i = pl.program_id(1)
    q0 = i * tq
    nkt = (q0 + tq + tk - 1) // tk
    qpos = q0 + lax.broadcasted_iota(jnp.int32, (1, tq), 1)
    krow = lax.broadcasted_iota(jnp.int32, (tk, tq), 0)

    npair = (nkt + 1) // 2

    def score_pair(jp, carry):
        for t in range(2):
            k0 = pl.multiple_of((2 * jp + t) * tk, tk)
            kk = kidx_ref[0, pl.ds(k0, tk), :]
            tot = jnp.zeros((tk, tq), F32)
            for h in range(IDX_HEADS):
                s = jnp.dot(kk, iqt_ref[0, h], preferred_element_type=F32)
                tot = tot + jnp.maximum(s, 0.0) * wit_ref[0, h:h + 1, :]
            sc = jnp.where(k0 + krow <= qpos, tot, -jnp.inf)
            sc_ref[pl.ds(k0, tk), :] = sc
            sb_ref[pl.ds(k0, tk), :] = sc.astype(BF16)
        return carry

    lax.fori_loop(0, npair, score_pair, 0)

    def over_tiles(body, init):
        return lax.fori_loop(0, npair,
                             lambda j, c: c + body(pl.multiple_of(j * (2 * tk), 2 * tk), 2 * tk), init)

    acc_rows = 4 * SUBLANES

    def count(pred):
        def body(k0, rows):
            hit = pred(sc_ref[pl.ds(k0, rows), :]).astype(jnp.int32)
            return jnp.sum(hit.reshape(rows // acc_rows, acc_rows, tq), axis=0)
        return jnp.sum(over_tiles(body, jnp.zeros((acc_rows, tq), jnp.int32)), axis=0, keepdims=True)

    def count_coarse(cand_b):
        one, zero = jnp.ones((), BF16), jnp.zeros((), BF16)

        def body(k0, rows):
            hit = jnp.where(sb_ref[pl.ds(k0, rows), :] >= cand_b, one, zero)
            h3 = hit.reshape(rows // acc_rows, acc_rows, tq)
            part = h3[0]
            for t in range(1, rows // acc_rows):
                part = part + h3[t]
            return part.astype(F32)
        return jnp.sum(over_tiles(body, jnp.zeros((acc_rows, tq), F32)), axis=0, keepdims=True)

    low_mask = jnp.int32(-(1 << (32 - COARSE_BITS)))

    def coarse(b, prefix):
        cand = prefix ^ (jnp.int32(1) << (31 - b))
        cand_b = lax.bitcast_convert_type(_key_to_bits(cand) & low_mask, F32).astype(BF16)
        return jnp.where(count_coarse(cand_b) >= topk, cand, prefix)

    prefix = lax.fori_loop(0, COARSE_BITS, coarse, jnp.full((1, tq), -2 ** 31, jnp.int32))
    key_p = _key_to_bits(_key_to_bits(prefix) & low_mask)

    def fine(_, st):
        lo, hi, at_lo = st
        mid = lo + ((hi - lo) >> 1)
        mid_f = _key_to_f32(mid)
        cnt = count(lambda s: s >= mid_f)
        ok = cnt >= topk
        return jnp.where(ok, mid, lo), jnp.where(ok, hi, mid), jnp.where(ok, cnt, at_lo)

    few_keys = qpos < topk
    n_fine = (3 * HALF_CELL + 2).bit_length()
    st = lax.fori_loop(0, n_fine - FINE_TAIL, fine, (key_p - (HALF_CELL + 1), key_p + (2 * HALF_CELL + 1),
                                                      jnp.full((1, tq), -1, jnp.int32)))
    all_settled = jnp.min(jnp.where((st[2] == topk) | few_keys, 1, 0)) > 0

    ntile = te_ref.shape[0]
    tile_row = lax.broadcasted_iota(jnp.int32, (ntile, 1), 0)

    def finish_search(st):
        lo = lax.fori_loop(0, FINE_TAIL, fine, st)[0]
        thr = jnp.where(few_keys, LOWEST, _key_to_f32(lo))
        need = (topk - count(lambda s: s > thr)).astype(F32)
        tt_ref[...] = jnp.zeros(tt_ref.shape, F32)

        def tile_ties(jp, carry):
            for t in range(2):
                k0 = pl.multiple_of((2 * jp + t) * tk, tk)
                eq = (sc_ref[pl.ds(k0, tk), :] == thr).astype(F32)
                tt_ref[pl.ds(2 * jp + t, 1), :] = jnp.sum(eq, axis=0, keepdims=True)
            return carry

        lax.fori_loop(0, npair, tile_ties, 0)
        ties = tt_ref[...]
        upto = lax.broadcasted_iota(jnp.int32, (ntile, ntile), 1) <= lax.broadcasted_iota(
            jnp.int32, (ntile, ntile), 0)
        after = jnp.dot(upto.astype(BF16), ties.astype(BF16), preferred_element_type=F32)
        before = after - ties
        thr_cut = jnp.where(few_keys, LOWEST, _key_to_f32(lo + 1))
        crossing = (before < need) & (after > need)
        flags = jnp.max(crossing.astype(jnp.int32), axis=1, keepdims=True)
        return thr, need, jnp.where(after <= need, thr, thr_cut), before, jnp.sum(flags << tile_row)

    def settled_search(st):
        thr = jnp.where(few_keys, LOWEST, _key_to_f32(st[0]))
        return (thr, jnp.zeros((1, tq), F32), jnp.broadcast_to(thr, (ntile, tq)), jnp.zeros((ntile, tq), F32),
                jnp.int32(0))

    thr, need_f, te, ties_before, crossing_bits = lax.cond(all_settled, settled_search, finish_search, st)
    te_ref[...] = te
    cb_ref[...] = ties_before

    acc_ref[...] = jnp.zeros(acc_ref.shape, F32)
    top_half = lax.broadcasted_iota(jnp.int32, (LANES, tq), 0) < ATT_HEAD_DIM
    for h in range(ATT_HEADS):
        pr = h // 2
        qp = qt_ref[0, pr * LANES:(pr + 1) * LANES, :]
        qh_ref[h] = jnp.where(top_half if h % 2 == 0 else ~top_half, qp, jnp.zeros_like(qp))

    def resolve(j, carry):
        @pl.when(((crossing_bits >> j) & 1) == 1)
        def _():
            k0 = pl.multiple_of(j * tk, tk)
            sc = sc_ref[pl.ds(k0, tk), :]
            eq = sc == thr
            before = jnp.dot(tri_ref[...], eq.astype(BF16), preferred_element_type=F32)
            sel = (sc > thr) | (eq & (before + cb_ref[pl.ds(j, 1), :] < need_f))
            sc_ref[pl.ds(k0, tk), :] = jnp.where(sel, jnp.inf, -jnp.inf)
        return carry

    lax.fori_loop(0, 2 * npair, resolve, 0)

    def mask_bias(j, k0):
        return jnp.where(sc_ref[pl.ds(k0, tk), :] >= te_ref[pl.ds(j, 1), :], 0.0, NEG)

    def finish(l_fin):
        rows = []
        for h in range(ATT_HEADS):
            r0 = (h % 2) * ATT_HEAD_DIM
            rows.append(acc_ref[h, r0:r0 + ATT_HEAD_DIM, :] / l_fin[h:h + 1, :])
        gate = gate_ref[...]
        o_ref[0] = (jnp.concatenate(rows, axis=0).T * (gate * jax.nn.sigmoid(gate))).astype(o_ref.dtype)

    if not online_max:
        def stage_logits(j, slot):
            k0 = pl.multiple_of(j * tk, tk)
            bias = mask_bias(j, k0)
            for h in range(ATT_HEADS):
                pr = h // 2
                s_ref[slot, h] = jnp.dot(k_ref[0, pl.ds(k0, tk), pr * LANES:(pr + 1) * LANES], qh_ref[h],
                                         preferred_element_type=F32) + bias

        def consume(j, slot, l_all):
            k0 = pl.multiple_of(j * tk, tk)
            ls = []
            for h in range(ATT_HEADS):
                pr = h // 2
                p = jnp.exp2(s_ref[slot, h])
                ls.append(jnp.sum(p, axis=0, keepdims=True))
                acc_ref[h] += jnp.dot(vt_ref[0, pr * LANES:(pr + 1) * LANES, pl.ds(k0, tk)], p.astype(BF16),
                                      preferred_element_type=F32)
            return l_all + jnp.concatenate(ls, axis=0)

        def attend_bounded(jp, l_all):
            stage_logits(2 * jp + 1, 1)
            l_all = consume(2 * jp, 0, l_all)
            stage_logits(jnp.minimum(2 * jp + 2, 2 * npair - 2), 0)
            return consume(2 * jp + 1, 1, l_all)

        stage_logits(0, 0)
        finish(lax.fori_loop(0, npair, attend_bounded, jnp.zeros((ATT_HEADS, tq), F32)))
        return

    def logits_pass(j, slot):
        k0 = pl.multiple_of(j * tk, tk)
        bias = mask_bias(j, k0)
        mx = []
        for h in range(ATT_HEADS):
            pr = h // 2
            s = jnp.dot(k_ref[0, pl.ds(k0, tk), pr * LANES:(pr + 1) * LANES], qh_ref[h],
                        preferred_element_type=F32) + bias
            s_ref[slot, h] = s
            mx.append(jnp.max(s, axis=0, keepdims=True))
        return jnp.concatenate(mx, axis=0)

    def value_pass(j, slot, m_all, l_all, mx):
        k0 = pl.multiple_of(j * tk, tk)
        m_new = jnp.maximum(m_all, mx)
        alpha = jnp.exp2(m_all - m_new)
        ls = []
        for h in range(ATT_HEADS):
            pr = h // 2
            p = jnp.exp2(s_ref[slot, h] - m_new[h:h + 1, :])
            ls.append(jnp.sum(p, axis=0, keepdims=True))
            pv = jnp.dot(vt_ref[0, pr * LANES:(pr + 1) * LANES, pl.ds(k0, tk)], p.astype(BF16),
                         preferred_element_type=F32)
            acc_ref[h] = alpha[h:h + 1, :] * acc_ref[h] + pv
        return m_new, alpha * l_all + jnp.concatenate(ls, axis=0)

    def attend(jp, carry):
        m_all, l_all, mx0 = carry
        mx1 = logits_pass(2 * jp + 1, 1)
        m_all, l_all = value_pass(2 * jp, 0, m_all, l_all, mx0)
        mx0 = logits_pass(jnp.minimum(2 * jp + 2, 2 * npair - 2), 0)
        m_all, l_all = value_pass(2 * jp + 1, 1, m_all, l_all, mx1)
        return m_all, l_all, mx0

    init = (jnp.full((ATT_HEADS, tq), NEG, F32), jnp.zeros((ATT_HEADS, tq), F32), logits_pass(0, 0))
    finish(lax.fori_loop(0, npair, attend, init)[1])


def _dsa_call(iqt, wit, qt, proj, tri, kidx, k, vt, b, l, tq, tk, online_max):
    nq = l // tq
    topk = min(TOPK_MAX, l // 4)
    kern = functools.partial(_dsa_kernel, tq=tq, tk=tk, topk=topk, online_max=online_max)
    s_stage = [pltpu.VMEM((2, ATT_HEADS, tk, tq), F32)]
    ntile = -(-(l // tk) // (4 * SUBLANES)) * (4 * SUBLANES)
    per_tile = [pltpu.VMEM((ntile, tq), F32)] * 3
    assert ntile <= 32 and (l // tk) % 2 == 0, "crossing tiles are flagged in one 32-bit word; tiles go in pairs"
    once = pl.Buffered(1)
    return pl.pallas_call(
        kern,
        grid=(b, nq),
        in_specs=[pl.BlockSpec((1, IDX_HEADS, IDX_K, tq), lambda bi, i: (bi, 0, 0, i)),
                  pl.BlockSpec((1, IDX_HEADS, tq), lambda bi, i: (bi, 0, i)),
                  pl.BlockSpec((1, ATT_WIDTH, tq), lambda bi, i: (bi, 0, i)),
                  pl.BlockSpec((tq, ATT_WIDTH), lambda bi, i: (bi * nq + i, 0)),
                  pl.BlockSpec((tk, tk), lambda bi, i: (0, 0), pipeline_mode=once),
                  pl.BlockSpec((1, l, IDX_K), lambda bi, i: (bi, 0, 0), pipeline_mode=once),
                  pl.BlockSpec((1, l, ATT_WIDTH), lambda bi, i: (bi, 0, 0), pipeline_mode=once),
                  pl.BlockSpec((1, ATT_WIDTH, l), lambda bi, i: (bi, 0, 0), pipeline_mode=once)],
        out_specs=pl.BlockSpec((1, tq, ATT_WIDTH), lambda bi, i: (bi, i, 0)),
        out_shape=jax.ShapeDtypeStruct((b, l, ATT_WIDTH), BF16),
        scratch_shapes=[pltpu.VMEM((l, tq), F32),
                        pltpu.VMEM((l, tq), BF16),
                        pltpu.VMEM((ATT_HEADS, LANES, tq), BF16),
                        pltpu.VMEM((ATT_HEADS, LANES, tq), F32)] + s_stage + per_tile,
        compiler_params=_cparams(("arbitrary", "arbitrary")),
        name="dsa_online_max" if online_max else "dsa",
    )(iqt, wit, qt, proj, tri, kidx, k, vt)


def _mm(a, b):
    return jnp.dot(a.astype(BF16), b.astype(BF16), preferred_element_type=F32)


def _mm_exact_lhs(a01, b):
    hi = b.astype(BF16)
    lo = (b - hi.astype(F32)).astype(BF16)
    a = a01.astype(BF16)
    return jnp.dot(a, hi, preferred_element_type=F32) + jnp.dot(a, lo, preferred_element_type=F32)


def _dn_kernel(dq_ref, dk_ref, dv_ref, dz_ref, dba_ref, cw_ref, avec_ref, bvec_ref, nw_ref,
               tril_ref, negl_ref, noteye_ref, o_ref, ext_ref, state_ref, *, rb):
    step = pl.program_id(1)
    halo = SUBLANES

    @pl.when(step == 0)
    def _():
        ext_ref[:, 0:halo, :] = jnp.zeros((3, halo, DN_WIDTH), F32)
        state_ref[...] = jnp.zeros(state_ref.shape, F32)

    def conv_silu(idx, src_ref):
        ext_ref[idx, halo:halo + rb, :] = src_ref[...]
        y = jnp.zeros((rb, DN_WIDTH), F32)
        for j in range(CONV_KERNEL):
            off = halo - (CONV_KERNEL - 1) + j
            y = y + ext_ref[idx, off:off + rb, :] * cw_ref[j:j + 1, idx * DN_WIDTH:(idx + 1) * DN_WIDTH]
        ext_ref[idx, 0:halo, :] = ext_ref[idx, rb:rb + halo, :]
        return y * jax.nn.sigmoid(y)

    def l2n(t):
        return t * lax.rsqrt(jnp.sum(t * t, axis=-1, keepdims=True) + EPS)

    qa = conv_silu(0, dq_ref)
    ka = conv_silu(1, dk_ref)
    va = conv_silu(2, dv_ref)

    dba = dba_ref[...]
    beta_all = jax.nn.sigmoid(dba)
    xg = dba + bvec_ref[...]
    softplus = jnp.maximum(xg, 0.0) + jnp.log1p(jnp.exp(-jnp.abs(xg)))
    g_all = -jnp.exp(avec_ref[...]) * softplus

    gc_all = _mm_exact_lhs(tril_ref[...], g_all)
    gc_rows = gc_all.T
    negl = negl_ref[...]
    noteye = noteye_ref[...]

    heads = range(DN_HEADS)
    sls = [slice(h * DN_HEAD_DIM, (h + 1) * DN_HEAD_DIM) for h in heads]
    qs = [l2n(qa[:, sls[h]]) * (DN_HEAD_DIM ** -0.5) for h in heads]
    ks = [l2n(ka[:, sls[h]]) for h in heads]
    gcs = [gc_all[:, DN_HEADS + h:DN_HEADS + h + 1] for h in heads]
    decays = [jnp.exp(gcs[h] - gc_rows[DN_HEADS + h:DN_HEADS + h + 1, :] + negl)
              for h in heads]
    k_betas = [ks[h] * beta_all[:, h:h + 1] for h in heads]
    kts = [ks[h].T for h in heads]
    nmats = [(_mm(k_betas[h], kts[h]) * decays[h] * noteye).astype(BF16) for h in heads]
    sols = [jnp.concatenate([va[:, sls[h]] * beta_all[:, h:h + 1], k_betas[h] * jnp.exp(gcs[h])], axis=-1)
            for h in heads]
    for it in range(6):
        sols = [sols[h] + jnp.dot(nmats[h], sols[h].astype(BF16), preferred_element_type=F32) for h in heads]
        if it < 5:
            nmats = [jnp.dot(nmats[h], nmats[h], preferred_element_type=F32).astype(BF16) for h in heads]
    folds = []
    for h in heads:
        intra = _mm(qs[h], kts[h]) * decays[h]
        fold = intra[:, 0:LANES]
        for t in range(1, rb // LANES):
            fold = fold + intra[:, t * LANES:(t + 1) * LANES]
        folds.append((fold + pltpu.roll(fold, CHUNK, 1))[:, 0:CHUNK])
    qgs = [qs[h] * jnp.exp(gcs[h]) for h in heads]
    states = [state_ref[h] for h in heads]
    outs = [[] for _ in heads]
    for ci in range(rb // CHUNK):
        cs = slice(ci * CHUNK, (ci + 1) * CHUNK)
        last = slice((ci + 1) * CHUNK - 1, (ci + 1) * CHUNK)
        v_news = [sols[h][cs, 0:DN_HEAD_DIM] - _mm(sols[h][cs, DN_HEAD_DIM:], states[h]) for h in heads]
        for h in heads:
            outs[h].append(_mm(qgs[h][cs], states[h]) + _mm(folds[h][cs], v_news[h]))
        kdecs = [ks[h][cs] * jnp.exp(gcs[h][last] - gcs[h][cs]) for h in heads]
        states = [states[h] * jnp.exp(gcs[h][last]) + _mm(kdecs[h].T, v_news[h]) for h in heads]
    for h in heads:
        state_ref[h] = states[h]
        o = jnp.concatenate(outs[h], axis=0)
        z = dz_ref[:, sls[h]]
        on = o * lax.rsqrt(jnp.mean(o * o, axis=-1, keepdims=True) + EPS) * nw_ref[...]
        o_ref[:, sls[h]] = (on * (z * jax.nn.sigmoid(z))).astype(o_ref.dtype)


def _dn_call(proj, conv_w, avec, bvec, norm_w, b, l, rb=256):
    nr = l // rb
    kern = functools.partial(_dn_kernel, rb=rb)

    def col(base):
        return lambda bi, i: (bi * nr + i, (base - COL_DQ) // DN_WIDTH)

    r = jnp.arange(rb)
    in_lower = (r[:, None] // CHUNK == r[None, :] // CHUNK) & (r[:, None] >= r[None, :])
    tril = in_lower.astype(BF16)
    negl = jnp.where(in_lower, 0.0, NEG).astype(F32)
    noteye = -(r[:, None] != r[None, :]).astype(F32)
    const = pl.BlockSpec((rb, rb), lambda bi, i: (0, 0))

    return pl.pallas_call(
        kern,
        grid=(b, nr),
        in_specs=[pl.BlockSpec((rb, DN_WIDTH), col(COL_DQ)),
                  pl.BlockSpec((rb, DN_WIDTH), col(COL_DK)),
                  pl.BlockSpec((rb, DN_WIDTH), col(COL_DV)),
                  pl.BlockSpec((rb, DN_WIDTH), col(COL_DZ)),
                  pl.BlockSpec((rb, LANES), lambda bi, i: (bi * nr + i, (COL_DBA - COL_DQ) // LANES)),
                  pl.BlockSpec((CONV_KERNEL, 3 * DN_WIDTH), lambda bi, i: (0, 0)),
                  pl.BlockSpec((1, LANES), lambda bi, i: (0, 0)),
                  pl.BlockSpec((1, LANES), lambda bi, i: (0, 0)),
                  pl.BlockSpec((1, DN_HEAD_DIM), lambda bi, i: (0, 0)),
                  const, const, const],
        out_specs=pl.BlockSpec((rb, DN_WIDTH), lambda bi, i: (bi * nr + i, 0)),
        out_shape=jax.ShapeDtypeStruct((b * l, DN_WIDTH), BF16),
        scratch_shapes=[pltpu.VMEM((3, rb + SUBLANES, DN_WIDTH), F32),
                        pltpu.VMEM((DN_HEADS, DN_HEAD_DIM, DN_HEAD_DIM), F32)],
        compiler_params=_cparams(("arbitrary", "arbitrary")),
        name="deltanet",
    )(proj, proj, proj, proj, proj, conv_w, avec, bvec, norm_w, tril, negl, noteye)


def _out_kernel(x_ref, oa_ref, od_ref, wa_ref, wd_ref, o_ref):
    acc = jnp.dot(oa_ref[...], wa_ref[...], preferred_element_type=F32)
    acc = acc + jnp.dot(od_ref[...], wd_ref[...], preferred_element_type=F32)
    o_ref[...] = x_ref[...] + acc


def _out_call(x2, oa, od, wa, wd, tm=512):
    n = x2.shape[0]
    return pl.pallas_call(
        _out_kernel,
        grid=(n // tm,),
        in_specs=[pl.BlockSpec((tm, D_MODEL), lambda i: (i, 0)),
                  pl.BlockSpec((tm, ATT_WIDTH), lambda i: (i, 0)),
                  pl.BlockSpec((tm, DN_WIDTH), lambda i: (i, 0)),
                  pl.BlockSpec((ATT_WIDTH, D_MODEL), lambda i: (0, 0)),
                  pl.BlockSpec((DN_WIDTH, D_MODEL), lambda i: (0, 0))],
        out_specs=pl.BlockSpec((tm, D_MODEL), lambda i: (i, 0)),
        out_shape=jax.ShapeDtypeStruct((n, D_MODEL), F32),
        compiler_params=_cparams(("arbitrary",)),
        name="out_proj",
    )(x2, oa, od, wa, wd)


def _layer(h, ln_w, w_in, attn_q_norm_w, attn_k_norm_w, idx_k_norm_w, idx_k_norm_b,
           dn_conv_w, dn_a_log, dn_dt_bias, dn_norm_w, w_out):
    b, l, _ = h.shape
    x2 = h.reshape(b * l, D_MODEL)

    n_ikw = IDX_HEAD_DIM + IDX_HEADS
    src_ikw = COL_IQ + IDX_HEADS * IDX_HEAD_DIM
    src_dn = src_ikw + n_ikw
    src_dba = src_dn + 4 * DN_WIDTH
    w_pad = jnp.concatenate(
        [w_in[:, :src_ikw], w_in[:, src_dn:src_dba],
         w_in[:, src_ikw:src_dn], jnp.zeros((D_MODEL, LANES - n_ikw), F32),
         w_in[:, src_dba:], jnp.zeros((D_MODEL, LANES - 2 * DN_HEADS), F32)], axis=1).astype(BF16)
    grp = jnp.arange(ATT_WIDTH) // ATT_HEAD_DIM
    gmat = (grp[:, None] == grp[None, :]).astype(BF16)
    wq_t = jnp.tile(attn_q_norm_w, ATT_HEADS)[None, :]
    wk_t = jnp.tile(attn_k_norm_w, ATT_HEADS)[None, :]
    lnw_p = jnp.pad(idx_k_norm_w, (0, LANES - IDX_HEAD_DIM))[None, :]
    lnb_p = jnp.pad(idx_k_norm_b, (0, LANES - IDX_HEAD_DIM))[None, :]
    avec = jnp.pad(dn_a_log, (DN_HEADS, LANES - 2 * DN_HEADS))[None, :]
    bvec = jnp.pad(dn_dt_bias, (DN_HEADS, LANES - 2 * DN_HEADS))[None, :]
    tq = min(256, l)
    tk = min(256, l)
    tri = (jnp.arange(tk)[None, :] < jnp.arange(tk)[:, None]).astype(BF16)

    qt, k, vt, gate, iqt, kidx, wit, dn_in = _proj_call(x2, ln_w[None, :], w_pad, gmat, wq_t, wk_t,
                                                        lnw_p, lnb_p, b, l)
    logit_bound = (ATT_HEAD_DIM ** 0.5 * LOG2E) * jnp.max(jnp.abs(attn_q_norm_w)) * jnp.max(jnp.abs(attn_k_norm_w))
    dsa_args = (iqt, wit, qt, gate, tri, kidx, k, vt)
    o_a = lax.cond(logit_bound * BF16_SLACK < LOGIT_SAFE,
                   lambda *a: _dsa_call(*a, b, l, tq, tk, online_max=False),
                   lambda *a: _dsa_call(*a, b, l, tq, tk, online_max=True), *dsa_args)
    o_d = _dn_call(dn_in, dn_conv_w, avec, bvec, dn_norm_w[None, :], b, l)
    out = _out_call(x2, o_a.reshape(b * l, ATT_WIDTH), o_d,
                    w_out[:ATT_WIDTH].astype(BF16), w_out[ATT_WIDTH:].astype(BF16))
    return out.reshape(b, l, D_MODEL)


def kernel(x, ln_w, w_in, attn_q_norm_w, attn_k_norm_w, idx_k_norm_w, idx_k_norm_b, dn_conv_w, dn_A_log,
           dn_dt_bias, dn_norm_w, w_out):
    h = x
    for layer in range(ln_w.shape[0]):
        h = _layer(h, ln_w[layer], w_in[layer], attn_q_norm_w[layer], attn_k_norm_w[layer],
                   idx_k_norm_w[layer], idx_k_norm_b[layer], dn_conv_w[layer], dn_A_log[layer],
                   dn_dt_bias[layer], dn_norm_w[layer], w_out[layer])
    return h
```

```python
import functools

import jax
import jax.numpy as jnp
from jax import lax
from jax.experimental import pallas as pl
from jax.experimental.pallas import tpu as pltpu

F32 = jnp.float32
BF16 = jnp.bfloat16

D_MODEL = 1024
ATT_HEADS = 8
ATT_HEAD_DIM = 64
ATT_WIDTH = ATT_HEADS * ATT_HEAD_DIM
IDX_HEADS = 8
IDX_HEAD_DIM = 64
TOPK_MAX = 256
DN_HEADS = 4
DN_HEAD_DIM = 128
DN_WIDTH = DN_HEADS * DN_HEAD_DIM
CONV_KERNEL = 4
CHUNK = 64
EPS = 1e-6
NEG = -1e30
LANES = 128
SUBLANES = 8
LOWEST = -3.0e38
LOG2E = 1.4426950408889634
LOGIT_SAFE = 60.0
BF16_SLACK = 1.02

COL_AQ, COL_AK, COL_AV, COL_AG = 0, 512, 1024, 1536
COL_IQ = 2048
COL_DQ, COL_DK, COL_DV, COL_DZ = 2560, 3072, 3584, 4096
COL_IKW = 4608
COL_DBA = 4736
D_PAD = 4864
IDX_K = 4 * IDX_HEAD_DIM

VMEM_LIMIT = 60 * 1024 * 1024


def _cparams(sem, flags=None):
    return pltpu.CompilerParams(dimension_semantics=sem, vmem_limit_bytes=VMEM_LIMIT, flags=flags)


def _proj_kernel(x_ref, lnw_ref, w_ref, g_ref, wq_ref, wk_ref, ilnw_ref, ilnb_ref,
                 qt_ref, k_ref, vt_ref, gate_ref, iqt_ref, kidx_ref, wit_ref, dn_ref):
    xf = x_ref[...]
    ms = jnp.mean(xf * xf, axis=-1, keepdims=True)
    hn = (xf * lax.rsqrt(ms + EPS) * lnw_ref[...]).astype(BF16)
    pa = jnp.dot(hn, w_ref[:, COL_AQ:COL_IQ], preferred_element_type=F32)
    _attn_prep(pa[:, COL_AQ:COL_AK], pa[:, COL_AK:COL_AV], pa[:, COL_AV:COL_AG],
               g_ref[...], wq_ref[...], wk_ref[...], qt_ref, k_ref, vt_ref)
    gate_ref[...] = pa[:, COL_AG:COL_IQ]
    pd = jnp.dot(hn, w_ref[:, COL_DQ:D_PAD], preferred_element_type=F32)
    dn_ref[...] = pd
    iq = jnp.dot(hn, w_ref[:, COL_IQ:COL_DQ], preferred_element_type=F32)
    _idx_prep(iq, pd[:, COL_IKW - COL_DQ:COL_IKW - COL_DQ + LANES], ilnw_ref[...], ilnb_ref[...],
              iqt_ref, kidx_ref, wit_ref)


def _proj_call(x2, ln_w, w_pad, gmat, wq_t, wk_t, lnw_p, lnb_p, b, l, tr=512):
    nr = l // tr
    wblk = ATT_WIDTH
    n_dn = D_PAD - COL_DQ

    def const(shape):
        return pl.BlockSpec(shape, lambda bi, i: (0, 0))

    return pl.pallas_call(
        _proj_kernel,
        grid=(b, nr),
        in_specs=[pl.BlockSpec((tr, D_MODEL), lambda bi, i: (bi * nr + i, 0)),
                  const((1, D_MODEL)),
                  pl.BlockSpec((D_MODEL, D_PAD), lambda bi, i: (0, 0), pipeline_mode=pl.Buffered(1)),
                  const((wblk, wblk)), const((1, wblk)), const((1, wblk)),
                  const((1, LANES)), const((1, LANES))],
        out_specs=[pl.BlockSpec((1, wblk, tr), lambda bi, i: (bi, 0, i)),
                   pl.BlockSpec((1, tr, wblk), lambda bi, i: (bi, i, 0)),
                   pl.BlockSpec((1, wblk, tr), lambda bi, i: (bi, 0, i)),
                   pl.BlockSpec((tr, wblk), lambda bi, i: (bi * nr + i, 0)),
                   pl.BlockSpec((1, IDX_HEADS, IDX_K, tr), lambda bi, i: (bi, 0, 0, i)),
                   pl.BlockSpec((1, tr, IDX_K), lambda bi, i: (bi, i, 0)),
                   pl.BlockSpec((1, IDX_HEADS, tr), lambda bi, i: (bi, 0, i)),
                   pl.BlockSpec((tr, n_dn), lambda bi, i: (bi * nr + i, 0))],
        out_shape=[jax.ShapeDtypeStruct((b, wblk, l), BF16),
                   jax.ShapeDtypeStruct((b, l, wblk), BF16),
                   jax.ShapeDtypeStruct((b, wblk, l), BF16),
                   jax.ShapeDtypeStruct((b * l, wblk), F32),
                   jax.ShapeDtypeStruct((b, IDX_HEADS, IDX_K, l), BF16),
                   jax.ShapeDtypeStruct((b, l, IDX_K), BF16),
                   jax.ShapeDtypeStruct((b, IDX_HEADS, l), F32),
                   jax.ShapeDtypeStruct((b * l, n_dn), F32)],
        compiler_params=_cparams(("arbitrary", "arbitrary")),
        name="proj",
    )(x2, ln_w, w_pad, gmat, wq_t, wk_t, lnw_p, lnb_p)


def _group_sumsq(x, g):
    sq = x * x
    hi = sq.astype(BF16)
    lo = (sq - hi.astype(F32)).astype(BF16)
    return (jnp.dot(hi, g, preferred_element_type=F32) + jnp.dot(lo, g, preferred_element_type=F32))


def _attn_prep(aq, ak, av, g, wq, wk, qt_ref, k_ref, vt_ref):
    inv_d = 1.0 / ATT_HEAD_DIM
    qn = aq * lax.rsqrt(_group_sumsq(aq, g) * inv_d + EPS) * wq
    kn = ak * lax.rsqrt(_group_sumsq(ak, g) * inv_d + EPS) * wk
    qt_ref[0] = (qn * (ATT_HEAD_DIM ** -0.5 * LOG2E)).T.astype(BF16)
    k_ref[0] = kn.astype(BF16)
    vt_ref[0] = av.T.astype(BF16)


def _hi_lo(x):
    hi = x.astype(BF16).astype(F32)
    return hi, x - hi


def _idx_prep(iq, ikw, lnw, lnb, iqt_ref, kidx_ref, wit_ref):
    tr = iq.shape[0]
    lane = lax.broadcasted_iota(jnp.int32, (tr, LANES), 1)
    low = lane < IDX_HEAD_DIM

    for j in range(IDX_HEADS // 2):
        d = iq[:, j * LANES:(j + 1) * LANES]
        r = pltpu.roll(d, IDX_HEAD_DIM, 1)
        for half, dup in enumerate((jnp.where(low, d, r), jnp.where(low, r, d))):
            hi, lo = _hi_lo(dup)
            h = 2 * j + half
            iqt_ref[0, h, 0:LANES, :] = jnp.where(low, hi, lo).T.astype(BF16)
            iqt_ref[0, h, LANES:2 * LANES, :] = jnp.where(low, hi, 0.0).T.astype(BF16)

    inv_d = 1.0 / IDX_HEAD_DIM
    mu = jnp.sum(jnp.where(low, ikw, 0.0), axis=-1, keepdims=True) * inv_d
    cen = jnp.where(low, ikw - mu, 0.0)
    var = jnp.sum(cen * cen, axis=-1, keepdims=True) * inv_d
    kn = jnp.where(low, cen * lax.rsqrt(var + EPS) * lnw + lnb, 0.0)
    hi, lo = _hi_lo(kn)
    kidx_ref[0, :, 0:LANES] = (hi + pltpu.roll(hi, IDX_HEAD_DIM, 1)).astype(BF16)
    kidx_ref[0, :, LANES:2 * LANES] = lo.astype(BF16)

    scale = (IDX_HEADS ** -0.5) * (IDX_HEAD_DIM ** -0.5)
    wit_ref[0] = (ikw * scale).T[IDX_HEAD_DIM:IDX_HEAD_DIM + IDX_HEADS, :]


_FLIP = 0x7FFFFFFF
COARSE_BITS = 16
HALF_CELL = 1 << (31 - COARSE_BITS)
FINE_TAIL = 4


def _key_to_bits(key):
    return jnp.where(key >= 0, key, key ^ _FLIP)


def _key_to_f32(key):
    return lax.bitcast_convert_type(_key_to_bits(key), F32)


def _dsa_kernel(iqt_ref, wit_ref, qt_ref, gate_ref, tri_ref, kidx_ref, k_ref, vt_ref, o_ref,
                sc_ref, sb_ref, qh_ref, acc_ref, s_ref, te_ref, cb_ref, tt_ref, *, tq, tk, topk, online_max):
    i = pl.program_id(1)
    q0 = i * tq
    nkt = (q0 + tq + tk - 1) // tk
    qpos = q0 + lax.broadcasted_iota(jnp.int32, (1, tq), 1)
    krow = lax.broadcasted_iota(jnp.int32, (tk, tq), 0)

    npair = (nkt + 1) // 2

    def score_pair(jp, carry):
        for t in range(2):
            k0 = pl.multiple_of((2 * jp + t) * tk, tk)
            kk = kidx_ref[0, pl.ds(k0, tk), :]
            tot = jnp.zeros((tk, tq), F32)
            for h in range(IDX_HEADS):
                s = jnp.dot(kk, iqt_ref[0, h], preferred_element_type=F32)
                tot = tot + jnp.maximum(s, 0.0) * wit_ref[0, h:h + 1, :]
            sc = jnp.where(k0 + krow <= qpos, tot, -jnp.inf)
            sc_ref[pl.ds(k0, tk), :] = sc
            sb_ref[pl.ds(k0, tk), :] = sc.astype(BF16)
        return carry

    lax.fori_loop(0, npair, score_pair, 0)

    def over_tiles(body, init):
        return lax.fori_loop(0, npair,
                             lambda j, c: c + body(pl.multiple_of(j * (2 * tk), 2 * tk), 2 * tk), init)

    acc_rows = 4 * SUBLANES

    def count(pred):
        def body(k0, rows):
            hit = pred(sc_ref[pl.ds(k0, rows), :]).astype(jnp.int32)
            return jnp.sum(hit.reshape(rows // acc_rows, acc_rows, tq), axis=0)
        return jnp.sum(over_tiles(body, jnp.zeros((acc_rows, tq), jnp.int32)), axis=0, keepdims=True)

    def count_coarse(cand_b):
        one, zero = jnp.ones((), BF16), jnp.zeros((), BF16)

        def body(k0, rows):
            hit = jnp.where(sb_ref[pl.ds(k0, rows), :] >= cand_b, one, zero)
            h3 = hit.reshape(rows // acc_rows, acc_rows, tq)
            part = h3[0]
            for t in range(1, rows // acc_rows):
                part = part + h3[t]
            return part.astype(F32)
        return jnp.sum(over_tiles(body, jnp.zeros((acc_rows, tq), F32)), axis=0, keepdims=True)

    low_mask = jnp.int32(-(1 << (32 - COARSE_BITS)))

    def coarse(b, prefix):
        cand = prefix ^ (jnp.int32(1) << (31 - b))
        cand_b = lax.bitcast_convert_type(_key_to_bits(cand) & low_mask, F32).astype(BF16)
        return jnp.where(count_coarse(cand_b) >= topk, cand, prefix)

    prefix = lax.fori_loop(0, COARSE_BITS, coarse, jnp.full((1, tq), -2 ** 31, jnp.int32))
    key_p = _key_to_bits(_key_to_bits(prefix) & low_mask)

    def fine(_, st):
        lo, hi, at_lo = st
        mid = lo + ((hi - lo) >> 1)
        mid_f = _key_to_f32(mid)
        cnt = count(lambda s: s >= mid_f)
        ok = cnt >= topk
        return jnp.where(ok, mid, lo), jnp.where(ok, hi, mid), jnp.where(ok, cnt, at_lo)

    few_keys = qpos < topk
    n_fine = (3 * HALF_CELL + 2).bit_length()
    st = lax.fori_loop(0, n_fine - FINE_TAIL, fine, (key_p - (HALF_CELL + 1), key_p + (2 * HALF_CELL + 1),
                                                      jnp.full((1, tq), -1, jnp.int32)))
    all_settled = jnp.min(jnp.where((st[2] == topk) | few_keys, 1, 0)) > 0

    acc_ref[...] = jnp.zeros(acc_ref.shape, F32)
    top_half = lax.broadcasted_iota(jnp.int32, (LANES, tq), 0) < ATT_HEAD_DIM
    for h in range(ATT_HEADS):
        pr = h // 2
        qp = qt_ref[0, pr * LANES:(pr + 1) * LANES, :]
        qh_ref[h] = jnp.where(top_half if h % 2 == 0 else ~top_half, qp, jnp.zeros_like(qp))

    ntile = te_ref.shape[0]
    tile_row = lax.broadcasted_iota(jnp.int32, (ntile, 1), 0)

    def finish_search(st):
        lo = lax.fori_loop(0, FINE_TAIL, fine, st)[0]
        thr = jnp.where(few_keys, LOWEST, _key_to_f32(lo))
        need = (topk - count(lambda s: s > thr)).astype(F32)
        tt_ref[...] = jnp.zeros(tt_ref.shape, F32)

        def tile_ties(jp, carry):
            for t in range(2):
                k0 = pl.multiple_of((2 * jp + t) * tk, tk)
                eq = (sc_ref[pl.ds(k0, tk), :] == thr).astype(F32)
                tt_ref[pl.ds(2 * jp + t, 1), :] = jnp.sum(eq, axis=0, keepdims=True)
            return carry

        lax.fori_loop(0, npair, tile_ties, 0)
        ties = tt_ref[...]
        upto = lax.broadcasted_iota(jnp.int32, (ntile, ntile), 1) <= lax.broadcasted_iota(
            jnp.int32, (ntile, ntile), 0)
        after = jnp.dot(upto.astype(BF16), ties.astype(BF16), preferred_element_type=F32)
        before = after - ties
        thr_cut = jnp.where(few_keys, LOWEST, _key_to_f32(lo + 1))
        crossing = (before < need) & (after > need)
        flags = jnp.max(crossing.astype(jnp.int32), axis=1, keepdims=True)
        return thr, need, jnp.where(after <= need, thr, thr_cut), before, jnp.sum(flags << tile_row)

    def settled_search(st):
        thr = jnp.where(few_keys, LOWEST, _key_to_f32(st[0]))
        return (thr, jnp.zeros((1, tq), F32), jnp.broadcast_to(thr, (ntile, tq)), jnp.zeros((ntile, tq), F32),
                jnp.int32(0))

    thr, need_f, te, ties_before, crossing_bits = lax.cond(all_settled, settled_search, finish_search, st)
    te_ref[...] = te
    cb_ref[...] = ties_before

    def resolve(j, carry):
        @pl.when(((crossing_bits >> j) & 1) == 1)
        def _():
            k0 = pl.multiple_of(j * tk, tk)
            sc = sc_ref[pl.ds(k0, tk), :]
            eq = sc == thr
            before = jnp.dot(tri_ref[...], eq.astype(BF16), preferred_element_type=F32)
            sel = (sc > thr) | (eq & (before + cb_ref[pl.ds(j, 1), :] < need_f))
            sc_ref[pl.ds(k0, tk), :] = jnp.where(sel, jnp.inf, -jnp.inf)
        return carry

    lax.fori_loop(0, 2 * npair, resolve, 0)

    def mask_bias(j, k0):
        return jnp.where(sc_ref[pl.ds(k0, tk), :] >= te_ref[pl.ds(j, 1), :], 0.0, NEG)

    def finish(l_fin):
        rows = []
        for h in range(ATT_HEADS):
            r0 = (h % 2) * ATT_HEAD_DIM
            rows.append(acc_ref[h, r0:r0 + ATT_HEAD_DIM, :] / l_fin[h:h + 1, :])
        gate = gate_ref[...]
        o_ref[0] = (jnp.concatenate(rows, axis=0).T * (gate * jax.nn.sigmoid(gate))).astype(o_ref.dtype)

    if not online_max:
        def stage_logits(j, slot):
            k0 = pl.multiple_of(j * tk, tk)
            bias = mask_bias(j, k0)
            for h in range(ATT_HEADS):
                pr = h // 2
                s_ref[slot, h] = jnp.dot(k_ref[0, pl.ds(k0, tk), pr * LANES:(pr + 1) * LANES], qh_ref[h],
                                         preferred_element_type=F32) + bias

        def consume(j, slot, l_all):
            k0 = pl.multiple_of(j * tk, tk)
            ls = []
            for h in range(ATT_HEADS):
                pr = h // 2
                p = jnp.exp2(s_ref[slot, h])
                ls.append(jnp.sum(p, axis=0, keepdims=True))
                acc_ref[h] += jnp.dot(vt_ref[0, pr * LANES:(pr + 1) * LANES, pl.ds(k0, tk)], p.astype(BF16),
                                      preferred_element_type=F32)
            return l_all + jnp.concatenate(ls, axis=0)

        def attend_bounded(jp, l_all):
            stage_logits(2 * jp + 1, 1)
            l_all = consume(2 * jp, 0, l_all)
            stage_logits(jnp.minimum(2 * jp + 2, 2 * npair - 2), 0)
            return consume(2 * jp + 1, 1, l_all)

        stage_logits(0, 0)
        finish(lax.fori_loop(0, npair, attend_bounded, jnp.zeros((ATT_HEADS, tq), F32)))
        return

    def logits_pass(j, slot):
        k0 = pl.multiple_of(j * tk, tk)
        bias = mask_bias(j, k0)
        mx = []
        for h in range(ATT_HEADS):
            pr = h // 2
            s = jnp.dot(k_ref[0, pl.ds(k0, tk), pr * LANES:(pr + 1) * LANES], qh_ref[h],
                        preferred_element_type=F32) + bias
            s_ref[slot, h] = s
            mx.append(jnp.max(s, axis=0, keepdims=True))
        return jnp.concatenate(mx, axis=0)

    def value_pass(j, slot, m_all, l_all, mx):
        k0 = pl.multiple_of(j * tk, tk)
        m_new = jnp.maximum(m_all, mx)
        alpha = jnp.exp2(m_all - m_new)
        ls = []
        for h in range(ATT_HEADS):
            pr = h // 2
            p = jnp.exp2(s_ref[slot, h] - m_new[h:h + 1, :])
            ls.append(jnp.sum(p, axis=0, keepdims=True))
            pv = jnp.dot(vt_ref[0, pr * LANES:(pr + 1) * LANES, pl.ds(k0, tk)], p.astype(BF16),
                         preferred_element_type=F32)
            acc_ref[h] = alpha[h:h + 1, :] * acc_ref[h] + pv
        return m_new, alpha * l_all + jnp.concatenate(ls, axis=0)

    def attend(jp, carry):
        m_all, l_all, mx0 = carry
        mx1 = logits_pass(2 * jp + 1, 1)
        m_all, l_all = value_pass(2 * jp, 0, m_all, l_all, mx0)
        mx0 = logits_pass(jnp.minimum(2 * jp + 2, 2 * npair - 2), 0)
        m_all, l_all = value_pass(2 * jp + 1, 1, m_all, l_all, mx1)
        return m_all, l_all, mx0

    init = (jnp.full((ATT_HEADS, tq), NEG, F32), jnp.zeros((ATT_HEADS, tq), F32), logits_pass(0, 0))
    finish(lax.fori_loop(0, npair, attend, init)[1])


def _dsa_call(iqt, wit, qt, proj, tri, kidx, k, vt, b, l, tq, tk, online_max):
    nq = l // tq
    topk = min(TOPK_MAX, l // 4)
    kern = functools.partial(_dsa_kernel, tq=tq, tk=tk, topk=topk, online_max=online_max)
    s_stage = [pltpu.VMEM((2, ATT_HEADS, tk, tq), F32)]
    ntile = -(-(l // tk) // (4 * SUBLANES)) * (4 * SUBLANES)
    per_tile = [pltpu.VMEM((ntile, tq), F32)] * 3
    assert ntile <= 32 and (l // tk) % 2 == 0, "crossing tiles are flagged in one 32-bit word; tiles go in pairs"
    once = pl.Buffered(1)
    return pl.pallas_call(
        kern,
        grid=(b, nq),
        in_specs=[pl.BlockSpec((1, IDX_HEADS, IDX_K, tq), lambda bi, i: (bi, 0, 0, i)),
                  pl.BlockSpec((1, IDX_HEADS, tq), lambda bi, i: (bi, 0, i)),
                  pl.BlockSpec((1, ATT_WIDTH, tq), lambda bi, i: (bi, 0, i)),
                  pl.BlockSpec((tq, ATT_WIDTH), lambda bi, i: (bi * nq + i, 0)),
                  pl.BlockSpec((tk, tk), lambda bi, i: (0, 0), pipeline_mode=once),
                  pl.BlockSpec((1, l, IDX_K), lambda bi, i: (bi, 0, 0), pipeline_mode=once),
                  pl.BlockSpec((1, l, ATT_WIDTH), lambda bi, i: (bi, 0, 0), pipeline_mode=once),
                  pl.BlockSpec((1, ATT_WIDTH, l), lambda bi, i: (bi, 0, 0), pipeline_mode=once)],
        out_specs=pl.BlockSpec((1, tq, ATT_WIDTH), lambda bi, i: (bi, i, 0)),
        out_shape=jax.ShapeDtypeStruct((b, l, ATT_WIDTH), BF16),
        scratch_shapes=[pltpu.VMEM((l, tq), F32),
                        pltpu.VMEM((l, tq), BF16),
                        pltpu.VMEM((ATT_HEADS, LANES, tq), BF16),
                        pltpu.VMEM((ATT_HEADS, LANES, tq), F32)] + s_stage + per_tile,
        compiler_params=_cparams(("arbitrary", "arbitrary")),
        name="dsa_online_max" if online_max else "dsa",
    )(iqt, wit, qt, proj, tri, kidx, k, vt)


def _mm(a, b):
    return jnp.dot(a.astype(BF16), b.astype(BF16), preferred_element_type=F32)


def _mm_exact_lhs(a01, b):
    hi = b.astype(BF16)
    lo = (b - hi.astype(F32)).astype(BF16)
    a = a01.astype(BF16)
    return jnp.dot(a, hi, preferred_element_type=F32) + jnp.dot(a, lo, preferred_element_type=F32)


def _dn_kernel(dq_ref, dk_ref, dv_ref, dz_ref, dba_ref, cw_ref, avec_ref, bvec_ref, nw_ref,
               tril_ref, negl_ref, noteye_ref, o_ref, ext_ref, state_ref, *, rb):
    step = pl.program_id(1)
    halo = SUBLANES

    @pl.when(step == 0)
    def _():
        ext_ref[:, 0:halo, :] = jnp.zeros((3, halo, DN_WIDTH), F32)
        state_ref[...] = jnp.zeros(state_ref.shape, F32)

    def conv_silu(idx, src_ref):
        ext_ref[idx, halo:halo + rb, :] = src_ref[...]
        y = jnp.zeros((rb, DN_WIDTH), F32)
        for j in range(CONV_KERNEL):
            off = halo - (CONV_KERNEL - 1) + j
            y = y + ext_ref[idx, off:off + rb, :] * cw_ref[j:j + 1, idx * DN_WIDTH:(idx + 1) * DN_WIDTH]
        ext_ref[idx, 0:halo, :] = ext_ref[idx, rb:rb + halo, :]
        return y * jax.nn.sigmoid(y)

    def l2n(t):
        return t * lax.rsqrt(jnp.sum(t * t, axis=-1, keepdims=True) + EPS)

    qa = conv_silu(0, dq_ref)
    ka = conv_silu(1, dk_ref)
    va = conv_silu(2, dv_ref)

    dba = dba_ref[...]
    beta_all = jax.nn.sigmoid(dba)
    xg = dba + bvec_ref[...]
    softplus = jnp.maximum(xg, 0.0) + jnp.log1p(jnp.exp(-jnp.abs(xg)))
    g_all = -jnp.exp(avec_ref[...]) * softplus

    gc_all = _mm_exact_lhs(tril_ref[...], g_all)
    gc_rows = gc_all.T
    negl = negl_ref[...]
    noteye = noteye_ref[...]

    heads = range(DN_HEADS)
    sls = [slice(h * DN_HEAD_DIM, (h + 1) * DN_HEAD_DIM) for h in heads]
    qs = [l2n(qa[:, sls[h]]) * (DN_HEAD_DIM ** -0.5) for h in heads]
    ks = [l2n(ka[:, sls[h]]) for h in heads]
    gcs = [gc_all[:, DN_HEADS + h:DN_HEADS + h + 1] for h in heads]
    decays = [jnp.exp(gcs[h] - gc_rows[DN_HEADS + h:DN_HEADS + h + 1, :] + negl)
              for h in heads]
    k_betas = [ks[h] * beta_all[:, h:h + 1] for h in heads]
    kts = [ks[h].T for h in heads]
    nmats = [(_mm(k_betas[h], kts[h]) * decays[h] * noteye).astype(BF16) for h in heads]
    sols = [jnp.concatenate([va[:, sls[h]] * beta_all[:, h:h + 1], k_betas[h] * jnp.exp(gcs[h])], axis=-1)
            for h in heads]
    for it in range(6):
        sols = [sols[h] + jnp.dot(nmats[h], sols[h].astype(BF16), preferred_element_type=F32) for h in heads]
        if it < 5:
            nmats = [jnp.dot(nmats[h], nmats[h], preferred_element_type=F32).astype(BF16) for h in heads]
    folds = []
    for h in heads:
        intra = _mm(qs[h], kts[h]) * decays[h]
        fold = intra[:, 0:LANES]
        for t in range(1, rb // LANES):
            fold = fold + intra[:, t * LANES:(t + 1) * LANES]
        folds.append((fold + pltpu.roll(fold, CHUNK, 1))[:, 0:CHUNK])
    qgs = [qs[h] * jnp.exp(gcs[h]) for h in heads]
    states = [state_ref[h] for h in heads]
    outs = [[] for _ in heads]
    for ci in range(rb // CHUNK):
        cs = slice(ci * CHUNK, (ci + 1) * CHUNK)
        last = slice((ci + 1) * CHUNK - 1, (ci + 1) * CHUNK)
        on_state = [_mm(jnp.concatenate([sols[h][cs, DN_HEAD_DIM:], qgs[h][cs]], axis=0), states[h]) for h in heads]
        v_news = [sols[h][cs, 0:DN_HEAD_DIM] - on_state[h][0:CHUNK] for h in heads]
        kdecs = [ks[h][cs] * jnp.exp(gcs[h][last] - gcs[h][cs]) for h in heads]
        on_v = [_mm(jnp.concatenate([folds[h][cs], kdecs[h].T], axis=0), v_news[h]) for h in heads]
        for h in heads:
            outs[h].append(on_state[h][CHUNK:] + on_v[h][0:CHUNK])
        states = [states[h] * jnp.exp(gcs[h][last]) + on_v[h][CHUNK:] for h in heads]
    for h in heads:
        state_ref[h] = states[h]
        o = jnp.concatenate(outs[h], axis=0)
        z = dz_ref[:, sls[h]]
        on = o * lax.rsqrt(jnp.mean(o * o, axis=-1, keepdims=True) + EPS) * nw_ref[...]
        o_ref[:, sls[h]] = (on * (z * jax.nn.sigmoid(z))).astype(o_ref.dtype)


def _dn_call(proj, conv_w, avec, bvec, norm_w, b, l, rb=256):
    nr = l // rb
    kern = functools.partial(_dn_kernel, rb=rb)

    def col(base):
        return lambda bi, i: (bi * nr + i, (base - COL_DQ) // DN_WIDTH)

    r = jnp.arange(rb)
    in_lower = (r[:, None] // CHUNK == r[None, :] // CHUNK) & (r[:, None] >= r[None, :])
    tril = in_lower.astype(BF16)
    negl = jnp.where(in_lower, 0.0, NEG).astype(F32)
    noteye = -(r[:, None] != r[None, :]).astype(F32)
    const = pl.BlockSpec((rb, rb), lambda bi, i: (0, 0))

    return pl.pallas_call(
        kern,
        grid=(b, nr),
        in_specs=[pl.BlockSpec((rb, DN_WIDTH), col(COL_DQ)),
                  pl.BlockSpec((rb, DN_WIDTH), col(COL_DK)),
                  pl.BlockSpec((rb, DN_WIDTH), col(COL_DV)),
                  pl.BlockSpec((rb, DN_WIDTH), col(COL_DZ)),
                  pl.BlockSpec((rb, LANES), lambda bi, i: (bi * nr + i, (COL_DBA - COL_DQ) // LANES)),
                  pl.BlockSpec((CONV_KERNEL, 3 * DN_WIDTH), lambda bi, i: (0, 0)),
                  pl.BlockSpec((1, LANES), lambda bi, i: (0, 0)),
                  pl.BlockSpec((1, LANES), lambda bi, i: (0, 0)),
                  pl.BlockSpec((1, DN_HEAD_DIM), lambda bi, i: (0, 0)),
                  const, const, const],
        out_specs=pl.BlockSpec((rb, DN_WIDTH), lambda bi, i: (bi * nr + i, 0)),
        out_shape=jax.ShapeDtypeStruct((b * l, DN_WIDTH), BF16),
        scratch_shapes=[pltpu.VMEM((3, rb + SUBLANES, DN_WIDTH), F32),
                        pltpu.VMEM((DN_HEADS, DN_HEAD_DIM, DN_HEAD_DIM), F32)],
        compiler_params=_cparams(("arbitrary", "arbitrary")),
        name="deltanet",
    )(proj, proj, proj, proj, proj, conv_w, avec, bvec, norm_w, tril, negl, noteye)


def _out_kernel(x_ref, oa_ref, od_ref, wa_ref, wd_ref, o_ref):
    acc = jnp.dot(oa_ref[...], wa_ref[...], preferred_element_type=F32)
    acc = acc + jnp.dot(od_ref[...], wd_ref[...], preferred_element_type=F32)
    o_ref[...] = x_ref[...] + acc


def _out_call(x2, oa, od, wa, wd, tm=512):
    n = x2.shape[0]
    return pl.pallas_call(
        _out_kernel,
        grid=(n // tm,),
        in_specs=[pl.BlockSpec((tm, D_MODEL), lambda i: (i, 0)),
                  pl.BlockSpec((tm, ATT_WIDTH), lambda i: (i, 0)),
                  pl.BlockSpec((tm, DN_WIDTH), lambda i: (i, 0)),
                  pl.BlockSpec((ATT_WIDTH, D_MODEL), lambda i: (0, 0)),
                  pl.BlockSpec((DN_WIDTH, D_MODEL), lambda i: (0, 0))],
        out_specs=pl.BlockSpec((tm, D_MODEL), lambda i: (i, 0)),
        out_shape=jax.ShapeDtypeStruct((n, D_MODEL), F32),
        compiler_params=_cparams(("arbitrary",)),
        name="out_proj",
    )(x2, oa, od, wa, wd)


def _layer(h, ln_w, w_in, attn_q_norm_w, attn_k_norm_w, idx_k_norm_w, idx_k_norm_b,
           dn_conv_w, dn_a_log, dn_dt_bias, dn_norm_w, w_out):
    b, l, _ = h.shape
    x2 = h.reshape(b * l, D_MODEL)

    n_ikw = IDX_HEAD_DIM + IDX_HEADS
    src_ikw = COL_IQ + IDX_HEADS * IDX_HEAD_DIM
    src_dn = src_ikw + n_ikw
    src_dba = src_dn + 4 * DN_WIDTH
    w_pad = jnp.concatenate(
        [w_in[:, :src_ikw], w_in[:, src_dn:src_dba],
         w_in[:, src_ikw:src_dn], jnp.zeros((D_MODEL, LANES - n_ikw), F32),
         w_in[:, src_dba:], jnp.zeros((D_MODEL, LANES - 2 * DN_HEADS), F32)], axis=1).astype(BF16)
    grp = jnp.arange(ATT_WIDTH) // ATT_HEAD_DIM
    gmat = (grp[:, None] == grp[None, :]).astype(BF16)
    wq_t = jnp.tile(attn_q_norm_w, ATT_HEADS)[None, :]
    wk_t = jnp.tile(attn_k_norm_w, ATT_HEADS)[None, :]
    lnw_p = jnp.pad(idx_k_norm_w, (0, LANES - IDX_HEAD_DIM))[None, :]
    lnb_p = jnp.pad(idx_k_norm_b, (0, LANES - IDX_HEAD_DIM))[None, :]
    avec = jnp.pad(dn_a_log, (DN_HEADS, LANES - 2 * DN_HEADS))[None, :]
    bvec = jnp.pad(dn_dt_bias, (DN_HEADS, LANES - 2 * DN_HEADS))[None, :]
    tq = min(256, l)
    tk = min(256, l)
    tri = (jnp.arange(tk)[None, :] < jnp.arange(tk)[:, None]).astype(BF16)

    qt, k, vt, gate, iqt, kidx, wit, dn_in = _proj_call(x2, ln_w[None, :], w_pad, gmat, wq_t, wk_t,
                                                        lnw_p, lnb_p, b, l)
    logit_bound = (ATT_HEAD_DIM ** 0.5 * LOG2E) * jnp.max(jnp.abs(attn_q_norm_w)) * jnp.max(jnp.abs(attn_k_norm_w))
    dsa_args = (iqt, wit, qt, gate, tri, kidx, k, vt)
    o_a = lax.cond(logit_bound * BF16_SLACK < LOGIT_SAFE,
                   lambda *a: _dsa_call(*a, b, l, tq, tk, online_max=False),
                   lambda *a: _dsa_call(*a, b, l, tq, tk, online_max=True), *dsa_args)
    o_d = _dn_call(dn_in, dn_conv_w, avec, bvec, dn_norm_w[None, :], b, l)
    out = _out_call(x2, o_a.reshape(b * l, ATT_WIDTH), o_d,
                    w_out[:ATT_WIDTH].astype(BF16), w_out[ATT_WIDTH:].astype(BF16))
    return out.reshape(b, l, D_MODEL)


def kernel(x, ln_w, w_in, attn_q_norm_w, attn_k_norm_w, idx_k_norm_w, idx_k_norm_b, dn_conv_w, dn_A_log,
           dn_dt_bias, dn_norm_w, w_out):
    h = x
    for layer in range(ln_w.shape[0]):
        h = _layer(h, ln_w[layer], w_in[layer], attn_q_norm_w[layer], attn_k_norm_w[layer],
                   idx_k_norm_w[layer], idx_k_norm_b[layer], dn_conv_w[layer], dn_A_log[layer],
                   dn_dt_bias[layer], dn_norm_w[layer], w_out[layer])
    return h
```

```python
import functools

import jax
import jax.numpy as jnp
from jax import lax
from jax.experimental import pallas as pl
from jax.experimental.pallas import tpu as pltpu

F32 = jnp.float32
BF16 = jnp.bfloat16

D_MODEL = 1024
ATT_HEADS = 8
ATT_HEAD_DIM = 64
ATT_WIDTH = ATT_HEADS * ATT_HEAD_DIM
IDX_HEADS = 8
IDX_HEAD_DIM = 64
TOPK_MAX = 256
DN_HEADS = 4
DN_HEAD_DIM = 128
DN_WIDTH = DN_HEADS * DN_HEAD_DIM
CONV_KERNEL = 4
CHUNK = 64
EPS = 1e-6
NEG = -1e30
LANES = 128
SUBLANES = 8
LOWEST = -3.0e38
LOG2E = 1.4426950408889634
LOGIT_SAFE = 60.0
BF16_SLACK = 1.02

COL_AQ, COL_AK, COL_AV, COL_AG = 0, 512, 1024, 1536
COL_IQ = 2048
COL_DQ, COL_DK, COL_DV, COL_DZ = 2560, 3072, 3584, 4096
COL_IKW = 4608
COL_DBA = 4736
D_PAD = 4864
IDX_K = 4 * IDX_HEAD_DIM

VMEM_LIMIT = 60 * 1024 * 1024


def _cparams(sem, flags=None):
    return pltpu.CompilerParams(dimension_semantics=sem, vmem_limit_bytes=VMEM_LIMIT, flags=flags)


def _proj_kernel(x_ref, lnw_ref, w_ref, g_ref, wq_ref, wk_ref, ilnw_ref, ilnb_ref,
                 qt_ref, k_ref, vt_ref, gate_ref, iqt_ref, kidx_ref, wit_ref, dn_ref):
    xf = x_ref[...]
    ms = jnp.mean(xf * xf, axis=-1, keepdims=True)
    hn = (xf * lax.rsqrt(ms + EPS) * lnw_ref[...]).astype(BF16)
    pa = jnp.dot(hn, w_ref[:, COL_AQ:COL_IQ], preferred_element_type=F32)
    _attn_prep(pa[:, COL_AQ:COL_AK], pa[:, COL_AK:COL_AV], pa[:, COL_AV:COL_AG],
               g_ref[...], wq_ref[...], wk_ref[...], qt_ref, k_ref, vt_ref)
    gate_ref[...] = pa[:, COL_AG:COL_IQ]
    pd = jnp.dot(hn, w_ref[:, COL_DQ:D_PAD], preferred_element_type=F32)
    dn_ref[...] = pd
    iq = jnp.dot(hn, w_ref[:, COL_IQ:COL_DQ], preferred_element_type=F32)
    _idx_prep(iq, pd[:, COL_IKW - COL_DQ:COL_IKW - COL_DQ + LANES], ilnw_ref[...], ilnb_ref[...],
              iqt_ref, kidx_ref, wit_ref)


def _proj_call(x2, ln_w, w_pad, gmat, wq_t, wk_t, lnw_p, lnb_p, b, l, tr=512):
    nr = l // tr
    wblk = ATT_WIDTH
    n_dn = D_PAD - COL_DQ

    def const(shape):
        return pl.BlockSpec(shape, lambda bi, i: (0, 0))

    return pl.pallas_call(
        _proj_kernel,
        grid=(b, nr),
        in_specs=[pl.BlockSpec((tr, D_MODEL), lambda bi, i: (bi * nr + i, 0)),
                  const((1, D_MODEL)),
                  pl.BlockSpec((D_MODEL, D_PAD), lambda bi, i: (0, 0), pipeline_mode=pl.Buffered(1)),
                  const((wblk, wblk)), const((1, wblk)), const((1, wblk)),
                  const((1, LANES)), const((1, LANES))],
        out_specs=[pl.BlockSpec((1, wblk, tr), lambda bi, i: (bi, 0, i)),
                   pl.BlockSpec((1, tr, wblk), lambda bi, i: (bi, i, 0)),
                   pl.BlockSpec((1, wblk, tr), lambda bi, i: (bi, 0, i)),
                   pl.BlockSpec((tr, wblk), lambda bi, i: (bi * nr + i, 0)),
                   pl.BlockSpec((1, IDX_HEADS, IDX_K, tr), lambda bi, i: (bi, 0, 0, i)),
                   pl.BlockSpec((1, tr, IDX_K), lambda bi, i: (bi, i, 0)),
                   pl.BlockSpec((1, IDX_HEADS, tr), lambda bi, i: (bi, 0, i)),
                   pl.BlockSpec((tr, n_dn), lambda bi, i: (bi * nr + i, 0))],
        out_shape=[jax.ShapeDtypeStruct((b, wblk, l), BF16),
                   jax.ShapeDtypeStruct((b, l, wblk), BF16),
                   jax.ShapeDtypeStruct((b, wblk, l), BF16),
                   jax.ShapeDtypeStruct((b * l, wblk), F32),
                   jax.ShapeDtypeStruct((b, IDX_HEADS, IDX_K, l), BF16),
                   jax.ShapeDtypeStruct((b, l, IDX_K), BF16),
                   jax.ShapeDtypeStruct((b, IDX_HEADS, l), F32),
                   jax.ShapeDtypeStruct((b * l, n_dn), F32)],
        compiler_params=_cparams(("arbitrary", "arbitrary")),
        name="proj",
    )(x2, ln_w, w_pad, gmat, wq_t, wk_t, lnw_p, lnb_p)


def _group_sumsq(x, g):
    sq = x * x
    hi = sq.astype(BF16)
    lo = (sq - hi.astype(F32)).astype(BF16)
    return (jnp.dot(hi, g, preferred_element_type=F32) + jnp.dot(lo, g, preferred_element_type=F32))


def _attn_prep(aq, ak, av, g, wq, wk, qt_ref, k_ref, vt_ref):
    inv_d = 1.0 / ATT_HEAD_DIM
    qn = aq * lax.rsqrt(_group_sumsq(aq, g) * inv_d + EPS) * wq
    kn = ak * lax.rsqrt(_group_sumsq(ak, g) * inv_d + EPS) * wk
    qt_ref[0] = (qn * (ATT_HEAD_DIM ** -0.5 * LOG2E)).T.astype(BF16)
    k_ref[0] = kn.astype(BF16)
    vt_ref[0] = av.T.astype(BF16)


def _hi_lo(x):
    hi = x.astype(BF16).astype(F32)
    return hi, x - hi


def _idx_prep(iq, ikw, lnw, lnb, iqt_ref, kidx_ref, wit_ref):
    tr = iq.shape[0]
    lane = lax.broadcasted_iota(jnp.int32, (tr, LANES), 1)
    low = lane < IDX_HEAD_DIM

    for j in range(IDX_HEADS // 2):
        d = iq[:, j * LANES:(j + 1) * LANES]
        r = pltpu.roll(d, IDX_HEAD_DIM, 1)
        for half, dup in enumerate((jnp.where(low, d, r), jnp.where(low, r, d))):
            hi, lo = _hi_lo(dup)
            h = 2 * j + half
            iqt_ref[0, h, 0:LANES, :] = jnp.where(low, hi, lo).T.astype(BF16)
            iqt_ref[0, h, LANES:2 * LANES, :] = jnp.where(low, hi, 0.0).T.astype(BF16)

    inv_d = 1.0 / IDX_HEAD_DIM
    mu = jnp.sum(jnp.where(low, ikw, 0.0), axis=-1, keepdims=True) * inv_d
    cen = jnp.where(low, ikw - mu, 0.0)
    var = jnp.sum(cen * cen, axis=-1, keepdims=True) * inv_d
    kn = jnp.where(low, cen * lax.rsqrt(var + EPS) * lnw + lnb, 0.0)
    hi, lo = _hi_lo(kn)
    kidx_ref[0, :, 0:LANES] = (hi + pltpu.roll(hi, IDX_HEAD_DIM, 1)).astype(BF16)
    kidx_ref[0, :, LANES:2 * LANES] = lo.astype(BF16)

    scale = (IDX_HEADS ** -0.5) * (IDX_HEAD_DIM ** -0.5)
    wit_ref[0] = (ikw * scale).T[IDX_HEAD_DIM:IDX_HEAD_DIM + IDX_HEADS, :]


_FLIP = 0x7FFFFFFF
COARSE_BITS = 16
HALF_CELL = 1 << (31 - COARSE_BITS)
FINE_TAIL = 4


def _key_to_bits(key):
    return jnp.where(key >= 0, key, key ^ _FLIP)


def _key_to_f32(key):
    return lax.bitcast_convert_type(_key_to_bits(key), F32)


def _dsa_kernel(iqt_ref, wit_ref, qt_ref, gate_ref, tri_ref, kidx_ref, k_ref, vt_ref, o_ref,
                sc_ref, sb_ref, qh_ref, acc_ref, s_ref, te_ref, cb_ref, tt_ref, *, tq, tk, topk, online_max):
    i = pl.program_id(1)
    q0 = i * tq
    nkt = (q0 + tq + tk - 1) // tk
    qpos = q0 + lax.broadcasted_iota(jnp.int32, (1, tq), 1)
    krow = lax.broadcasted_iota(jnp.int32, (tk, tq), 0)

    npair = (nkt + 1) // 2

    def score_pair(jp, carry):
        for t in range(2):
            k0 = pl.multiple_of((2 * jp + t) * tk, tk)
            kk = kidx_ref[0, pl.ds(k0, tk), :]
            tot = jnp.zeros((tk, tq), F32)
            for h in range(IDX_HEADS):
                s = jnp.dot(kk, iqt_ref[0, h], preferred_element_type=F32)
                tot = tot + jnp.maximum(s, 0.0) * wit_ref[0, h:h + 1, :]
            sc = jnp.where(k0 + krow <= qpos, tot, -jnp.inf)
            sc_ref[pl.ds(k0, tk), :] = sc
            sb_ref[pl.ds(k0, tk), :] = sc.astype(BF16)
        return carry

    lax.fori_loop(0, npair, score_pair, 0)

    def over_tiles(body, init):
        return lax.fori_loop(0, npair,
                             lambda j, c: c + body(pl.multiple_of(j * (2 * tk), 2 * tk), 2 * tk), init)

    acc_rows = 4 * SUBLANES

    def count(pred):
        def body(k0, rows):
            hit = pred(sc_ref[pl.ds(k0, rows), :]).astype(jnp.int32)
            return jnp.sum(hit.reshape(rows // acc_rows, acc_rows, tq), axis=0)
        return jnp.sum(over_tiles(body, jnp.zeros((acc_rows, tq), jnp.int32)), axis=0, keepdims=True)

    def count_coarse(cand_b):
        one, zero = jnp.ones((), BF16), jnp.zeros((), BF16)

        def body(k0, rows):
            hit = jnp.where(sb_ref[pl.ds(k0, rows), :] >= cand_b, one, zero)
            h3 = hit.reshape(rows // acc_rows, acc_rows, tq)
            part = h3[0]
            for t in range(1, rows // acc_rows):
                part = part + h3[t]
            return part.astype(F32)
        return jnp.sum(over_tiles(body, jnp.zeros((acc_rows, tq), F32)), axis=0, keepdims=True)

    low_mask = jnp.int32(-(1 << (32 - COARSE_BITS)))

    def coarse(b, prefix):
        cand = prefix ^ (jnp.int32(1) << (31 - b))
        cand_b = lax.bitcast_convert_type(_key_to_bits(cand) & low_mask, F32).astype(BF16)
        return jnp.where(count_coarse(cand_b) >= topk, cand, prefix)

    prefix = lax.fori_loop(0, COARSE_BITS, coarse, jnp.full((1, tq), -2 ** 31, jnp.int32))
    key_p = _key_to_bits(_key_to_bits(prefix) & low_mask)

    def fine(_, st):
        lo, hi, at_lo = st
        mid = lo + ((hi - lo) >> 1)
        mid_f = _key_to_f32(mid)
        cnt = count(lambda s: s >= mid_f)
        ok = cnt >= topk
        return jnp.where(ok, mid, lo), jnp.where(ok, hi, mid), jnp.where(ok, cnt, at_lo)

    few_keys = qpos < topk
    n_fine = (3 * HALF_CELL + 2).bit_length()
    st = lax.fori_loop(0, n_fine - FINE_TAIL, fine, (key_p - (HALF_CELL + 1), key_p + (2 * HALF_CELL + 1),
                                                      jnp.full((1, tq), -1, jnp.int32)))
    all_settled = jnp.min(jnp.where((st[2] == topk) | few_keys, 1, 0)) > 0

    acc_ref[...] = jnp.zeros(acc_ref.shape, F32)
    top_half = lax.broadcasted_iota(jnp.int32, (LANES, tq), 0) < ATT_HEAD_DIM
    for h in range(ATT_HEADS):
        pr = h // 2
        qp = qt_ref[0, pr * LANES:(pr + 1) * LANES, :]
        qh_ref[h] = jnp.where(top_half if h % 2 == 0 else ~top_half, qp, jnp.zeros_like(qp))

    ntile = te_ref.shape[0]
    tile_row = lax.broadcasted_iota(jnp.int32, (ntile, 1), 0)

    def finish_search(st):
        lo = lax.fori_loop(0, FINE_TAIL, fine, st)[0]
        thr = jnp.where(few_keys, LOWEST, _key_to_f32(lo))
        need = (topk - count(lambda s: s > thr)).astype(F32)
        tt_ref[...] = jnp.zeros(tt_ref.shape, F32)

        def tile_ties(jp, carry):
            for t in range(2):
                k0 = pl.multiple_of((2 * jp + t) * tk, tk)
                eq = (sc_ref[pl.ds(k0, tk), :] == thr).astype(F32)
                tt_ref[pl.ds(2 * jp + t, 1), :] = jnp.sum(eq, axis=0, keepdims=True)
            return carry

        lax.fori_loop(0, npair, tile_ties, 0)
        ties = tt_ref[...]
        upto = lax.broadcasted_iota(jnp.int32, (ntile, ntile), 1) <= lax.broadcasted_iota(
            jnp.int32, (ntile, ntile), 0)
        after = jnp.dot(upto.astype(BF16), ties.astype(BF16), preferred_element_type=F32)
        before = after - ties
        thr_cut = jnp.where(few_keys, LOWEST, _key_to_f32(lo + 1))
        crossing = (before < need) & (after > need)
        flags = jnp.max(crossing.astype(jnp.int32), axis=1, keepdims=True)
        return thr, need, jnp.where(after <= need, thr, thr_cut), before, jnp.sum(flags << tile_row)

    def settled_search(st):
        thr = jnp.where(few_keys, LOWEST, _key_to_f32(st[0]))
        return (thr, jnp.zeros((1, tq), F32), jnp.broadcast_to(thr, (ntile, tq)), jnp.zeros((ntile, tq), F32),
                jnp.int32(0))

    thr, need_f, te, ties_before, crossing_bits = lax.cond(all_settled, settled_search, finish_search, st)
    te_ref[...] = te
    cb_ref[...] = ties_before

    def resolve(j, carry):
        @pl.when(((crossing_bits >> j) & 1) == 1)
        def _():
            k0 = pl.multiple_of(j * tk, tk)
            sc = sc_ref[pl.ds(k0, tk), :]
            eq = sc == thr
            before = jnp.dot(tri_ref[...], eq.astype(BF16), preferred_element_type=F32)
            sel = (sc > thr) | (eq & (before + cb_ref[pl.ds(j, 1), :] < need_f))
            sc_ref[pl.ds(k0, tk), :] = jnp.where(sel, jnp.inf, -jnp.inf)
        return carry

    lax.fori_loop(0, 2 * npair, resolve, 0)

    def mask_bias(j, k0):
        return jnp.where(sc_ref[pl.ds(k0, tk), :] >= te_ref[pl.ds(j, 1), :], 0.0, NEG)

    def finish(l_fin):
        rows = []
        for h in range(ATT_HEADS):
            r0 = (h % 2) * ATT_HEAD_DIM
            rows.append(acc_ref[h, r0:r0 + ATT_HEAD_DIM, :] / l_fin[h:h + 1, :])
        gate = gate_ref[...]
        o_ref[0] = (jnp.concatenate(rows, axis=0).T * (gate * jax.nn.sigmoid(gate))).astype(o_ref.dtype)

    if not online_max:
        def stage_logits(j, slot):
            k0 = pl.multiple_of(j * tk, tk)
            bias = mask_bias(j, k0)
            for h in range(ATT_HEADS):
                pr = h // 2
                s_ref[slot, h] = jnp.dot(k_ref[0, pl.ds(k0, tk), pr * LANES:(pr + 1) * LANES], qh_ref[h],
                                         preferred_element_type=F32) + bias

        def consume(j, slot, l_all):
            k0 = pl.multiple_of(j * tk, tk)
            ls = []
            for h in range(ATT_HEADS):
                pr = h // 2
                p = jnp.exp2(s_ref[slot, h])
                ls.append(jnp.sum(p, axis=0, keepdims=True))
                acc_ref[h] += jnp.dot(vt_ref[0, pr * LANES:(pr + 1) * LANES, pl.ds(k0, tk)], p.astype(BF16),
                                      preferred_element_type=F32)
            return l_all + jnp.concatenate(ls, axis=0)

        def attend_bounded(jp, l_all):
            stage_logits(2 * jp + 1, 1)
            l_all = consume(2 * jp, 0, l_all)
            stage_logits(jnp.minimum(2 * jp + 2, 2 * npair - 2), 0)
            return consume(2 * jp + 1, 1, l_all)

        stage_logits(0, 0)
        finish(lax.fori_loop(0, npair, attend_bounded, jnp.zeros((ATT_HEADS, tq), F32)))
        return

    def logits_pass(j, slot):
        k0 = pl.multiple_of(j * tk, tk)
        bias = mask_bias(j, k0)
        mx = []
        for h in range(ATT_HEADS):
            pr = h // 2
            s = jnp.dot(k_ref[0, pl.ds(k0, tk), pr * LANES:(pr + 1) * LANES], qh_ref[h],
                        preferred_element_type=F32) + bias
            s_ref[slot, h] = s
            mx.append(jnp.max(s, axis=0, keepdims=True))
        return jnp.concatenate(mx, axis=0)

    def value_pass(j, slot, m_all, l_all, mx):
        k0 = pl.multiple_of(j * tk, tk)
        m_new = jnp.maximum(m_all, mx)
        alpha = jnp.exp2(m_all - m_new)
        ls = []
        for h in range(ATT_HEADS):
            pr = h // 2
            p = jnp.exp2(s_ref[slot, h] - m_new[h:h + 1, :])
            ls.append(jnp.sum(p, axis=0, keepdims=True))
            pv = jnp.dot(vt_ref[0, pr * LANES:(pr + 1) * LANES, pl.ds(k0, tk)], p.astype(BF16),
                         preferred_element_type=F32)
            acc_ref[h] = alpha[h:h + 1, :] * acc_ref[h] + pv
        return m_new, alpha * l_all + jnp.concatenate(ls, axis=0)

    def attend(jp, carry):
        m_all, l_all, mx0 = carry
        mx1 = logits_pass(2 * jp + 1, 1)
        m_all, l_all = value_pass(2 * jp, 0, m_all, l_all, mx0)
        mx0 = logits_pass(jnp.minimum(2 * jp + 2, 2 * npair - 2), 0)
        m_all, l_all = value_pass(2 * jp + 1, 1, m_all, l_all, mx1)
        return m_all, l_all, mx0

    init = (jnp.full((ATT_HEADS, tq), NEG, F32), jnp.zeros((ATT_HEADS, tq), F32), logits_pass(0, 0))
    finish(lax.fori_loop(0, npair, attend, init)[1])


def _dsa_call(iqt, wit, qt, proj, tri, kidx, k, vt, b, l, tq, tk, online_max):
    nq = l // tq
    topk = min(TOPK_MAX, l // 4)
    kern = functools.partial(_dsa_kernel, tq=tq, tk=tk, topk=topk, online_max=online_max)
    s_stage = [pltpu.VMEM((2, ATT_HEADS, tk, tq), F32)]
    ntile = -(-(l // tk) // (4 * SUBLANES)) * (4 * SUBLANES)
    per_tile = [pltpu.VMEM((ntile, tq), F32)] * 3
    assert ntile <= 32 and (l // tk) % 2 == 0, "crossing tiles are flagged in one 32-bit word; tiles go in pairs"
    once = pl.Buffered(1)
    return pl.pallas_call(
        kern,
        grid=(b, nq),
        in_specs=[pl.BlockSpec((1, IDX_HEADS, IDX_K, tq), lambda bi, i: (bi, 0, 0, i)),
                  pl.BlockSpec((1, IDX_HEADS, tq), lambda bi, i: (bi, 0, i)),
                  pl.BlockSpec((1, ATT_WIDTH, tq), lambda bi, i: (bi, 0, i)),
                  pl.BlockSpec((tq, ATT_WIDTH), lambda bi, i: (bi * nq + i, 0)),
                  pl.BlockSpec((tk, tk), lambda bi, i: (0, 0), pipeline_mode=once),
                  pl.BlockSpec((1, l, IDX_K), lambda bi, i: (bi, 0, 0), pipeline_mode=once),
                  pl.BlockSpec((1, l, ATT_WIDTH), lambda bi, i: (bi, 0, 0), pipeline_mode=once),
                  pl.BlockSpec((1, ATT_WIDTH, l), lambda bi, i: (bi, 0, 0), pipeline_mode=once)],
        out_specs=pl.BlockSpec((1, tq, ATT_WIDTH), lambda bi, i: (bi, i, 0)),
        out_shape=jax.ShapeDtypeStruct((b, l, ATT_WIDTH), BF16),
        scratch_shapes=[pltpu.VMEM((l, tq), F32),
                        pltpu.VMEM((l, tq), BF16),
                        pltpu.VMEM((ATT_HEADS, LANES, tq), BF16),
                        pltpu.VMEM((ATT_HEADS, LANES, tq), F32)] + s_stage + per_tile,
        compiler_params=_cparams(("arbitrary", "arbitrary")),
        name="dsa_online_max" if online_max else "dsa",
    )(iqt, wit, qt, proj, tri, kidx, k, vt)


def _mm(a, b):
    return jnp.dot(a.astype(BF16), b.astype(BF16), preferred_element_type=F32)


def _mm_exact_lhs(a01, b):
    hi = b.astype(BF16)
    lo = (b - hi.astype(F32)).astype(BF16)
    a = a01.astype(BF16)
    return jnp.dot(a, hi, preferred_element_type=F32) + jnp.dot(a, lo, preferred_element_type=F32)


def _dn_kernel(dq_ref, dk_ref, dv_ref, dz_ref, dba_ref, cw_ref, avec_ref, bvec_ref, nw_ref,
               tril_ref, negl_ref, noteye_ref, o_ref, ext_ref, state_ref, *, rb):
    step = pl.program_id(1)
    halo = SUBLANES

    @pl.when(step == 0)
    def _():
        ext_ref[:, 0:halo, :] = jnp.zeros((3, halo, DN_WIDTH), F32)
        state_ref[...] = jnp.zeros(state_ref.shape, F32)

    def conv_silu(idx, src_ref):
        ext_ref[idx, halo:halo + rb, :] = src_ref[...]
        xe = ext_ref[idx]
        y = jnp.zeros((rb, DN_WIDTH), F32)
        for j in range(CONV_KERNEL):
            back = CONV_KERNEL - 1 - j
            xs = xe if back == 0 else pltpu.roll(xe, back, 0)
            y = y + xs[halo:halo + rb] * cw_ref[j:j + 1, idx * DN_WIDTH:(idx + 1) * DN_WIDTH]
        ext_ref[idx, 0:halo, :] = ext_ref[idx, rb:rb + halo, :]
        return y * jax.nn.sigmoid(y)

    def l2n(t):
        return t * lax.rsqrt(jnp.sum(t * t, axis=-1, keepdims=True) + EPS)

    qa = conv_silu(0, dq_ref)
    ka = conv_silu(1, dk_ref)
    va = conv_silu(2, dv_ref)

    dba = dba_ref[...]
    beta_all = jax.nn.sigmoid(dba)
    xg = dba + bvec_ref[...]
    softplus = jnp.maximum(xg, 0.0) + jnp.log1p(jnp.exp(-jnp.abs(xg)))
    g_all = -jnp.exp(avec_ref[...]) * softplus

    gc_all = _mm_exact_lhs(tril_ref[...], g_all)
    gc_rows = gc_all.T
    negl = negl_ref[...]
    noteye = noteye_ref[...]

    heads = range(DN_HEADS)
    sls = [slice(h * DN_HEAD_DIM, (h + 1) * DN_HEAD_DIM) for h in heads]
    qs = [l2n(qa[:, sls[h]]) * (DN_HEAD_DIM ** -0.5) for h in heads]
    ks = [l2n(ka[:, sls[h]]) for h in heads]
    gcs = [gc_all[:, DN_HEADS + h:DN_HEADS + h + 1] for h in heads]
    decays = [jnp.exp(gcs[h] - gc_rows[DN_HEADS + h:DN_HEADS + h + 1, :] + negl)
              for h in heads]
    k_betas = [ks[h] * beta_all[:, h:h + 1] for h in heads]
    kts = [ks[h].T for h in heads]
    nmats = [(_mm(k_betas[h], kts[h]) * decays[h] * noteye).astype(BF16) for h in heads]
    sols = [jnp.concatenate([va[:, sls[h]] * beta_all[:, h:h + 1], k_betas[h] * jnp.exp(gcs[h])], axis=-1)
            for h in heads]
    for it in range(6):
        sols = [sols[h] + jnp.dot(nmats[h], sols[h].astype(BF16), preferred_element_type=F32) for h in heads]
        if it < 5:
            nmats = [jnp.dot(nmats[h], nmats[h], preferred_element_type=F32).astype(BF16) for h in heads]
    folds = []
    for h in heads:
        intra = _mm(qs[h], kts[h]) * decays[h]
        fold = intra[:, 0:LANES]
        for t in range(1, rb // LANES):
            fold = fold + intra[:, t * LANES:(t + 1) * LANES]
        folds.append((fold + pltpu.roll(fold, CHUNK, 1))[:, 0:CHUNK])
    qgs = [qs[h] * jnp.exp(gcs[h]) for h in heads]
    states = [state_ref[h] for h in heads]
    outs = [[] for _ in heads]
    for ci in range(rb // CHUNK):
        cs = slice(ci * CHUNK, (ci + 1) * CHUNK)
        last = slice((ci + 1) * CHUNK - 1, (ci + 1) * CHUNK)
        on_state = [_mm(jnp.concatenate([sols[h][cs, DN_HEAD_DIM:], qgs[h][cs]], axis=0), states[h]) for h in heads]
        v_news = [sols[h][cs, 0:DN_HEAD_DIM] - on_state[h][0:CHUNK] for h in heads]
        kdecs = [ks[h][cs] * jnp.exp(gcs[h][last] - gcs[h][cs]) for h in heads]
        on_v = [_mm(jnp.concatenate([folds[h][cs], kdecs[h].T], axis=0), v_news[h]) for h in heads]
        for h in heads:
            outs[h].append(on_state[h][CHUNK:] + on_v[h][0:CHUNK])
        states = [states[h] * jnp.exp(gcs[h][last]) + on_v[h][CHUNK:] for h in heads]
    for h in heads:
        state_ref[h] = states[h]
        o = jnp.concatenate(outs[h], axis=0)
        z = dz_ref[:, sls[h]]
        on = o * lax.rsqrt(jnp.mean(o * o, axis=-1, keepdims=True) + EPS) * nw_ref[...]
        o_ref[:, sls[h]] = (on * (z * jax.nn.sigmoid(z))).astype(o_ref.dtype)


def _dn_call(proj, conv_w, avec, bvec, norm_w, b, l, rb=256):
    nr = l // rb
    kern = functools.partial(_dn_kernel, rb=rb)

    def col(base):
        return lambda bi, i: (bi * nr + i, (base - COL_DQ) // DN_WIDTH)

    r = jnp.arange(rb)
    in_lower = (r[:, None] // CHUNK == r[None, :] // CHUNK) & (r[:, None] >= r[None, :])
    tril = in_lower.astype(BF16)
    negl = jnp.where(in_lower, 0.0, NEG).astype(F32)
    noteye = -(r[:, None] != r[None, :]).astype(F32)
    const = pl.BlockSpec((rb, rb), lambda bi, i: (0, 0))

    return pl.pallas_call(
        kern,
        grid=(b, nr),
        in_specs=[pl.BlockSpec((rb, DN_WIDTH), col(COL_DQ)),
                  pl.BlockSpec((rb, DN_WIDTH), col(COL_DK)),
                  pl.BlockSpec((rb, DN_WIDTH), col(COL_DV)),
                  pl.BlockSpec((rb, DN_WIDTH), col(COL_DZ)),
                  pl.BlockSpec((rb, LANES), lambda bi, i: (bi * nr + i, (COL_DBA - COL_DQ) // LANES)),
                  pl.BlockSpec((CONV_KERNEL, 3 * DN_WIDTH), lambda bi, i: (0, 0)),
                  pl.BlockSpec((1, LANES), lambda bi, i: (0, 0)),
                  pl.BlockSpec((1, LANES), lambda bi, i: (0, 0)),
                  pl.BlockSpec((1, DN_HEAD_DIM), lambda bi, i: (0, 0)),
                  const, const, const],
        out_specs=pl.BlockSpec((rb, DN_WIDTH), lambda bi, i: (bi * nr + i, 0)),
        out_shape=jax.ShapeDtypeStruct((b * l, DN_WIDTH), BF16),
        scratch_shapes=[pltpu.VMEM((3, rb + SUBLANES, DN_WIDTH), F32),
                        pltpu.VMEM((DN_HEADS, DN_HEAD_DIM, DN_HEAD_DIM), F32)],
        compiler_params=_cparams(("arbitrary", "arbitrary")),
        name="deltanet",
    )(proj, proj, proj, proj, proj, conv_w, avec, bvec, norm_w, tril, negl, noteye)


def _out_kernel(x_ref, oa_ref, od_ref, wa_ref, wd_ref, o_ref):
    acc = jnp.dot(oa_ref[...], wa_ref[...], preferred_element_type=F32)
    acc = acc + jnp.dot(od_ref[...], wd_ref[...], preferred_element_type=F32)
    o_ref[...] = x_ref[...] + acc


def _out_call(x2, oa, od, wa, wd, tm=512):
    n = x2.shape[0]
    return pl.pallas_call(
        _out_kernel,
        grid=(n // tm,),
        in_specs=[pl.BlockSpec((tm, D_MODEL), lambda i: (i, 0)),
                  pl.BlockSpec((tm, ATT_WIDTH), lambda i: (i, 0)),
                  pl.BlockSpec((tm, DN_WIDTH), lambda i: (i, 0)),
                  pl.BlockSpec((ATT_WIDTH, D_MODEL), lambda i: (0, 0)),
                  pl.BlockSpec((DN_WIDTH, D_MODEL), lambda i: (0, 0))],
        out_specs=pl.BlockSpec((tm, D_MODEL), lambda i: (i, 0)),
        out_shape=jax.ShapeDtypeStruct((n, D_MODEL), F32),
        compiler_params=_cparams(("arbitrary",)),
        name="out_proj",
    )(x2, oa, od, wa, wd)


def _layer(h, ln_w, w_in, attn_q_norm_w, attn_k_norm_w, idx_k_norm_w, idx_k_norm_b,
           dn_conv_w, dn_a_log, dn_dt_bias, dn_norm_w, w_out):
    b, l, _ = h.shape
    x2 = h.reshape(b * l, D_MODEL)

    n_ikw = IDX_HEAD_DIM + IDX_HEADS
    src_ikw = COL_IQ + IDX_HEADS * IDX_HEAD_DIM
    src_dn = src_ikw + n_ikw
    src_dba = src_dn + 4 * DN_WIDTH
    w_pad = jnp.concatenate(
        [w_in[:, :src_ikw], w_in[:, src_dn:src_dba],
         w_in[:, src_ikw:src_dn], jnp.zeros((D_MODEL, LANES - n_ikw), F32),
         w_in[:, src_dba:], jnp.zeros((D_MODEL, LANES - 2 * DN_HEADS), F32)], axis=1).astype(BF16)
    grp = jnp.arange(ATT_WIDTH) // ATT_HEAD_DIM
    gmat = (grp[:, None] == grp[None, :]).astype(BF16)
    wq_t = jnp.tile(attn_q_norm_w, ATT_HEADS)[None, :]
    wk_t = jnp.tile(attn_k_norm_w, ATT_HEADS)[None, :]
    lnw_p = jnp.pad(idx_k_norm_w, (0, LANES - IDX_HEAD_DIM))[None, :]
    lnb_p = jnp.pad(idx_k_norm_b, (0, LANES - IDX_HEAD_DIM))[None, :]
    avec = jnp.pad(dn_a_log, (DN_HEADS, LANES - 2 * DN_HEADS))[None, :]
    bvec = jnp.pad(dn_dt_bias, (DN_HEADS, LANES - 2 * DN_HEADS))[None, :]
    tq = min(256, l)
    tk = min(256, l)
    tri = (jnp.arange(tk)[None, :] < jnp.arange(tk)[:, None]).astype(BF16)

    qt, k, vt, gate, iqt, kidx, wit, dn_in = _proj_call(x2, ln_w[None, :], w_pad, gmat, wq_t, wk_t,
                                                        lnw_p, lnb_p, b, l)
    logit_bound = (ATT_HEAD_DIM ** 0.5 * LOG2E) * jnp.max(jnp.abs(attn_q_norm_w)) * jnp.max(jnp.abs(attn_k_norm_w))
    dsa_args = (iqt, wit, qt, gate, tri, kidx, k, vt)
    o_a = lax.cond(logit_bound * BF16_SLACK < LOGIT_SAFE,
                   lambda *a: _dsa_call(*a, b, l, tq, tk, online_max=False),
                   lambda *a: _dsa_call(*a, b, l, tq, tk, online_max=True), *dsa_args)
    o_d = _dn_call(dn_in, dn_conv_w, avec, bvec, dn_norm_w[None, :], b, l)
    out = _out_call(x2, o_a.reshape(b * l, ATT_WIDTH), o_d,
                    w_out[:ATT_WIDTH].astype(BF16), w_out[ATT_WIDTH:].astype(BF16))
    return out.reshape(b, l, D_MODEL)


def kernel(x, ln_w, w_in, attn_q_norm_w, attn_k_norm_w, idx_k_norm_w, idx_k_norm_b, dn_conv_w, dn_A_log,
           dn_dt_bias, dn_norm_w, w_out):
    h = x
    for layer in range(ln_w.shape[0]):
        h = _layer(h, ln_w[layer], w_in[layer], attn_q_norm_w[layer], attn_k_norm_w[layer],
                   idx_k_norm_w[layer], idx_k_norm_b[layer], dn_conv_w[layer], dn_A_log[layer],
                   dn_dt_bias[layer], dn_norm_w[layer], w_out[layer])
    return h
```

```python
import functools

import jax
import jax.numpy as jnp
from jax import lax
from jax.experimental import pallas as pl
from jax.experimental.pallas import tpu as pltpu

F32 = jnp.float32
BF16 = jnp.bfloat16

D_MODEL = 1024
ATT_HEADS = 8
ATT_HEAD_DIM = 64
ATT_WIDTH = ATT_HEADS * ATT_HEAD_DIM
IDX_HEADS = 8
IDX_HEAD_DIM = 64
TOPK_MAX = 256
DN_HEADS = 4
DN_HEAD_DIM = 128
DN_WIDTH = DN_HEADS * DN_HEAD_DIM
CONV_KERNEL = 4
CHUNK = 64
EPS = 1e-6
NEG = -1e30
LANES = 128
SUBLANES = 8
LOWEST = -3.0e38
LOG2E = 1.4426950408889634
LOGIT_SAFE = 60.0
BF16_SLACK = 1.02

COL_AQ, COL_AK, COL_AV, COL_AG = 0, 512, 1024, 1536
COL_IQ = 2048
COL_DQ, COL_DK, COL_DV, COL_DZ = 2560, 3072, 3584, 4096
COL_IKW = 4608
COL_DBA = 4736
D_PAD = 4864
IDX_K = 4 * IDX_HEAD_DIM

VMEM_LIMIT = 60 * 1024 * 1024


def _cparams(sem, flags=None):
    return pltpu.CompilerParams(dimension_semantics=sem, vmem_limit_bytes=VMEM_LIMIT, flags=flags)


def _proj_kernel(x_ref, lnw_ref, w_ref, g_ref, wq_ref, wk_ref, ilnw_ref, ilnb_ref,
                 qt_ref, k_ref, vt_ref, gate_ref, iqt_ref, kidx_ref, wit_ref, dn_ref):
    xf = x_ref[...]
    ms = jnp.mean(xf * xf, axis=-1, keepdims=True)
    hn = (xf * lax.rsqrt(ms + EPS) * lnw_ref[...]).astype(BF16)
    pa = jnp.dot(hn, w_ref[:, COL_AQ:COL_IQ], preferred_element_type=F32)
    _attn_prep(pa[:, COL_AQ:COL_AK], pa[:, COL_AK:COL_AV], pa[:, COL_AV:COL_AG],
               g_ref[...], wq_ref[...], wk_ref[...], qt_ref, k_ref, vt_ref)
    gate_ref[...] = pa[:, COL_AG:COL_IQ]
    pd = jnp.dot(hn, w_ref[:, COL_DQ:D_PAD], preferred_element_type=F32)
    dn_ref[...] = pd
    iq = jnp.dot(hn, w_ref[:, COL_IQ:COL_DQ], preferred_element_type=F32)
    _idx_prep(iq, pd[:, COL_IKW - COL_DQ:COL_IKW - COL_DQ + LANES], ilnw_ref[...], ilnb_ref[...],
              iqt_ref, kidx_ref, wit_ref)


def _proj_call(x2, ln_w, w_pad, gmat, wq_t, wk_t, lnw_p, lnb_p, b, l, tr=512):
    nr = l // tr
    wblk = ATT_WIDTH
    n_dn = D_PAD - COL_DQ

    def const(shape):
        return pl.BlockSpec(shape, lambda bi, i: (0, 0))

    return pl.pallas_call(
        _proj_kernel,
        grid=(b, nr),
        in_specs=[pl.BlockSpec((tr, D_MODEL), lambda bi, i: (bi * nr + i, 0)),
                  const((1, D_MODEL)),
                  pl.BlockSpec((D_MODEL, D_PAD), lambda bi, i: (0, 0), pipeline_mode=pl.Buffered(1)),
                  const((wblk, wblk)), const((1, wblk)), const((1, wblk)),
                  const((1, LANES)), const((1, LANES))],
        out_specs=[pl.BlockSpec((1, wblk, tr), lambda bi, i: (bi, 0, i)),
                   pl.BlockSpec((1, tr, wblk), lambda bi, i: (bi, i, 0)),
                   pl.BlockSpec((1, wblk, tr), lambda bi, i: (bi, 0, i)),
                   pl.BlockSpec((tr, wblk), lambda bi, i: (bi * nr + i, 0)),
                   pl.BlockSpec((1, IDX_HEADS, IDX_K, tr), lambda bi, i: (bi, 0, 0, i)),
                   pl.BlockSpec((1, tr, IDX_K), lambda bi, i: (bi, i, 0)),
                   pl.BlockSpec((1, IDX_HEADS, tr), lambda bi, i: (bi, 0, i)),
                   pl.BlockSpec((tr, n_dn), lambda bi, i: (bi * nr + i, 0))],
        out_shape=[jax.ShapeDtypeStruct((b, wblk, l), BF16),
                   jax.ShapeDtypeStruct((b, l, wblk), BF16),
                   jax.ShapeDtypeStruct((b, wblk, l), BF16),
                   jax.ShapeDtypeStruct((b * l, wblk), F32),
                   jax.ShapeDtypeStruct((b, IDX_HEADS, IDX_K, l), BF16),
                   jax.ShapeDtypeStruct((b, l, IDX_K), BF16),
                   jax.ShapeDtypeStruct((b, IDX_HEADS, l), F32),
                   jax.ShapeDtypeStruct((b * l, n_dn), F32)],
        compiler_params=_cparams(("arbitrary", "arbitrary")),
        name="proj",
    )(x2, ln_w, w_pad, gmat, wq_t, wk_t, lnw_p, lnb_p)


def _group_sumsq(x, g):
    sq = x * x
    hi = sq.astype(BF16)
    lo = (sq - hi.astype(F32)).astype(BF16)
    return (jnp.dot(hi, g, preferred_element_type=F32) + jnp.dot(lo, g, preferred_element_type=F32))


def _attn_prep(aq, ak, av, g, wq, wk, qt_ref, k_ref, vt_ref):
    inv_d = 1.0 / ATT_HEAD_DIM
    qn = aq * lax.rsqrt(_group_sumsq(aq, g) * inv_d + EPS) * wq
    kn = ak * lax.rsqrt(_group_sumsq(ak, g) * inv_d + EPS) * wk
    qt_ref[0] = (qn * (ATT_HEAD_DIM ** -0.5 * LOG2E)).T.astype(BF16)
    k_ref[0] = kn.astype(BF16)
    vt_ref[0] = av.T.astype(BF16)


def _hi_lo(x):
    hi = x.astype(BF16).astype(F32)
    return hi, x - hi


def _idx_prep(iq, ikw, lnw, lnb, iqt_ref, kidx_ref, wit_ref):
    tr = iq.shape[0]
    lane = lax.broadcasted_iota(jnp.int32, (tr, LANES), 1)
    low = lane < IDX_HEAD_DIM

    for j in range(IDX_HEADS // 2):
        d = iq[:, j * LANES:(j + 1) * LANES]
        r = pltpu.roll(d, IDX_HEAD_DIM, 1)
        for half, dup in enumerate((jnp.where(low, d, r), jnp.where(low, r, d))):
            hi, lo = _hi_lo(dup)
            h = 2 * j + half
            iqt_ref[0, h, 0:LANES, :] = jnp.where(low, hi, lo).T.astype(BF16)
            iqt_ref[0, h, LANES:2 * LANES, :] = jnp.where(low, hi, 0.0).T.astype(BF16)

    inv_d = 1.0 / IDX_HEAD_DIM
    mu = jnp.sum(jnp.where(low, ikw, 0.0), axis=-1, keepdims=True) * inv_d
    cen = jnp.where(low, ikw - mu, 0.0)
    var = jnp.sum(cen * cen, axis=-1, keepdims=True) * inv_d
    kn = jnp.where(low, cen * lax.rsqrt(var + EPS) * lnw + lnb, 0.0)
    hi, lo = _hi_lo(kn)
    kidx_ref[0, :, 0:LANES] = (hi + pltpu.roll(hi, IDX_HEAD_DIM, 1)).astype(BF16)
    kidx_ref[0, :, LANES:2 * LANES] = lo.astype(BF16)

    scale = (IDX_HEADS ** -0.5) * (IDX_HEAD_DIM ** -0.5)
    wit_ref[0] = (ikw * scale).T[IDX_HEAD_DIM:IDX_HEAD_DIM + IDX_HEADS, :]


_FLIP = 0x7FFFFFFF
COARSE_BITS = 16
HALF_CELL = 1 << (31 - COARSE_BITS)
FINE_TAIL = 4


def _key_to_bits(key):
    return jnp.where(key >= 0, key, key ^ _FLIP)


def _key_to_f32(key):
    return lax.bitcast_convert_type(_key_to_bits(key), F32)


def _dsa_kernel(iqt_ref, wit_ref, qt_ref, gate_ref, tri_ref, kidx_ref, k_ref, vt_ref, o_ref,
                sc_ref, sb_ref, qh_ref, acc_ref, s_ref, te_ref, cb_ref, tt_ref, p_ref, ps_ref,
                *, tq, tk, topk, online_max):
    i = pl.program_id(1)
    q0 = i * tq
    nkt = (q0 + tq + tk - 1) // tk
    qpos = q0 + lax.broadcasted_iota(jnp.int32, (1, tq), 1)
    krow = lax.broadcasted_iota(jnp.int32, (tk, tq), 0)

    npair = (nkt + 1) // 2

    def score_pair(jp, carry):
        for t in range(2):
            k0 = pl.multiple_of((2 * jp + t) * tk, tk)
            kk = kidx_ref[0, pl.ds(k0, tk), :]
            tot = jnp.zeros((tk, tq), F32)
            for h in range(IDX_HEADS):
                s = jnp.dot(kk, iqt_ref[0, h], preferred_element_type=F32)
                tot = tot + jnp.maximum(s, 0.0) * wit_ref[0, h:h + 1, :]
            sc = jnp.where(k0 + krow <= qpos, tot, -jnp.inf)
            sc_ref[pl.ds(k0, tk), :] = sc
            sb_ref[pl.ds(k0, tk), :] = sc.astype(BF16)
        return carry

    lax.fori_loop(0, npair, score_pair, 0)

    def over_tiles(body, init):
        return lax.fori_loop(0, npair,
                             lambda j, c: c + body(pl.multiple_of(j * (2 * tk), 2 * tk), 2 * tk), init)

    acc_rows = 4 * SUBLANES

    def count(pred):
        def body(k0, rows):
            hit = pred(sc_ref[pl.ds(k0, rows), :]).astype(jnp.int32)
            return jnp.sum(hit.reshape(rows // acc_rows, acc_rows, tq), axis=0)
        return jnp.sum(over_tiles(body, jnp.zeros((acc_rows, tq), jnp.int32)), axis=0, keepdims=True)

    def count_coarse(cand_b):
        one, zero = jnp.ones((), BF16), jnp.zeros((), BF16)

        def body(k0, rows):
            hit = jnp.where(sb_ref[pl.ds(k0, rows), :] >= cand_b, one, zero)
            h3 = hit.reshape(rows // acc_rows, acc_rows, tq)
            part = h3[0]
            for t in range(1, rows // acc_rows):
                part = part + h3[t]
            return part.astype(F32)
        return jnp.sum(over_tiles(body, jnp.zeros((acc_rows, tq), F32)), axis=0, keepdims=True)

    low_mask = jnp.int32(-(1 << (32 - COARSE_BITS)))

    def coarse(b, prefix):
        cand = prefix ^ (jnp.int32(1) << (31 - b))
        cand_b = lax.bitcast_convert_type(_key_to_bits(cand) & low_mask, F32).astype(BF16)
        return jnp.where(count_coarse(cand_b) >= topk, cand, prefix)

    prefix = lax.fori_loop(0, COARSE_BITS, coarse, jnp.full((1, tq), -2 ** 31, jnp.int32))
    key_p = _key_to_bits(_key_to_bits(prefix) & low_mask)

    def fine(_, st):
        lo, hi, at_lo = st
        mid = lo + ((hi - lo) >> 1)
        mid_f = _key_to_f32(mid)
        cnt = count(lambda s: s >= mid_f)
        ok = cnt >= topk
        return jnp.where(ok, mid, lo), jnp.where(ok, hi, mid), jnp.where(ok, cnt, at_lo)

    few_keys = qpos < topk
    n_fine = (3 * HALF_CELL + 2).bit_length()
    st = lax.fori_loop(0, n_fine - FINE_TAIL, fine, (key_p - (HALF_CELL + 1), key_p + (2 * HALF_CELL + 1),
                                                      jnp.full((1, tq), -1, jnp.int32)))
    all_settled = jnp.min(jnp.where((st[2] == topk) | few_keys, 1, 0)) > 0

    acc_ref[...] = jnp.zeros(acc_ref.shape, F32)
    top_half = lax.broadcasted_iota(jnp.int32, (LANES, tq), 0) < ATT_HEAD_DIM
    for h in range(ATT_HEADS):
        pr = h // 2
        qp = qt_ref[0, pr * LANES:(pr + 1) * LANES, :]
        qh_ref[h] = jnp.where(top_half if h % 2 == 0 else ~top_half, qp, jnp.zeros_like(qp))

    ntile = te_ref.shape[0]
    tile_row = lax.broadcasted_iota(jnp.int32, (ntile, 1), 0)

    def finish_search(st):
        lo = lax.fori_loop(0, FINE_TAIL, fine, st)[0]
        thr = jnp.where(few_keys, LOWEST, _key_to_f32(lo))
        need = (topk - count(lambda s: s > thr)).astype(F32)
        tt_ref[...] = jnp.zeros(tt_ref.shape, F32)

        def tile_ties(jp, carry):
            for t in range(2):
                k0 = pl.multiple_of((2 * jp + t) * tk, tk)
                eq = (sc_ref[pl.ds(k0, tk), :] == thr).astype(F32)
                tt_ref[pl.ds(2 * jp + t, 1), :] = jnp.sum(eq, axis=0, keepdims=True)
            return carry

        lax.fori_loop(0, npair, tile_ties, 0)
        ties = tt_ref[...]
        upto = lax.broadcasted_iota(jnp.int32, (ntile, ntile), 1) <= lax.broadcasted_iota(
            jnp.int32, (ntile, ntile), 0)
        after = jnp.dot(upto.astype(BF16), ties.astype(BF16), preferred_element_type=F32)
        before = after - ties
        thr_cut = jnp.where(few_keys, LOWEST, _key_to_f32(lo + 1))
        crossing = (before < need) & (after > need)
        flags = jnp.max(crossing.astype(jnp.int32), axis=1, keepdims=True)
        return thr, need, jnp.where(after <= need, thr, thr_cut), before, jnp.sum(flags << tile_row)

    def settled_search(st):
        thr = jnp.where(few_keys, LOWEST, _key_to_f32(st[0]))
        return (thr, jnp.zeros((1, tq), F32), jnp.broadcast_to(thr, (ntile, tq)), jnp.zeros((ntile, tq), F32),
                jnp.int32(0))

    thr, need_f, te, ties_before, crossing_bits = lax.cond(all_settled, settled_search, finish_search, st)
    te_ref[...] = te
    cb_ref[...] = ties_before

    def resolve(j, carry):
        @pl.when(((crossing_bits >> j) & 1) == 1)
        def _():
            k0 = pl.multiple_of(j * tk, tk)
            sc = sc_ref[pl.ds(k0, tk), :]
            eq = sc == thr
            before = jnp.dot(tri_ref[...], eq.astype(BF16), preferred_element_type=F32)
            sel = (sc > thr) | (eq & (before + cb_ref[pl.ds(j, 1), :] < need_f))
            sc_ref[pl.ds(k0, tk), :] = jnp.where(sel, jnp.inf, -jnp.inf)
        return carry

    lax.fori_loop(0, 2 * npair, resolve, 0)

    def mask_bias(j, k0):
        return jnp.where(sc_ref[pl.ds(k0, tk), :] >= te_ref[pl.ds(j, 1), :], 0.0, NEG)

    def finish(l_fin):
        rows = []
        for h in range(ATT_HEADS):
            r0 = (h % 2) * ATT_HEAD_DIM
            rows.append(acc_ref[h, r0:r0 + ATT_HEAD_DIM, :] / l_fin[h:h + 1, :])
        gate = gate_ref[...]
        o_ref[0] = (jnp.concatenate(rows, axis=0).T * (gate * jax.nn.sigmoid(gate))).astype(o_ref.dtype)

    if not online_max:
        def stage_logits(j, slot):
            k0 = pl.multiple_of(j * tk, tk)
            bias = mask_bias(j, k0)
            ls = []
            for h in range(ATT_HEADS):
                pr = h // 2
                p = jnp.exp2(jnp.dot(k_ref[0, pl.ds(k0, tk), pr * LANES:(pr + 1) * LANES], qh_ref[h],
                                     preferred_element_type=F32) + bias)
                ls.append(jnp.sum(p, axis=0, keepdims=True))
                p_ref[slot, h] = p.astype(BF16)
            ps_ref[slot] = jnp.concatenate(ls, axis=0)

        def consume(j, slot, l_all):
            k0 = pl.multiple_of(j * tk, tk)
            for h in range(ATT_HEADS):
                pr = h // 2
                acc_ref[h] += jnp.dot(vt_ref[0, pr * LANES:(pr + 1) * LANES, pl.ds(k0, tk)], p_ref[slot, h],
                                      preferred_element_type=F32)
            return l_all + ps_ref[slot]

        def attend_bounded(jp, l_all):
            stage_logits(2 * jp + 1, 1)
            l_all = consume(2 * jp, 0, l_all)
            stage_logits(jnp.minimum(2 * jp + 2, 2 * npair - 2), 0)
            return consume(2 * jp + 1, 1, l_all)

        stage_logits(0, 0)
        finish(lax.fori_loop(0, npair, attend_bounded, jnp.zeros((ATT_HEADS, tq), F32)))
        return

    def logits_pass(j, slot):
        k0 = pl.multiple_of(j * tk, tk)
        bias = mask_bias(j, k0)
        mx = []
        for h in range(ATT_HEADS):
            pr = h // 2
            s = jnp.dot(k_ref[0, pl.ds(k0, tk), pr * LANES:(pr + 1) * LANES], qh_ref[h],
                        preferred_element_type=F32) + bias
            s_ref[slot, h] = s
            mx.append(jnp.max(s, axis=0, keepdims=True))
        return jnp.concatenate(mx, axis=0)

    def value_pass(j, slot, m_all, l_all, mx):
        k0 = pl.multiple_of(j * tk, tk)
        m_new = jnp.maximum(m_all, mx)
        alpha = jnp.exp2(m_all - m_new)
        ls = []
        for h in range(ATT_HEADS):
            pr = h // 2
            p = jnp.exp2(s_ref[slot, h] - m_new[h:h + 1, :])
            ls.append(jnp.sum(p, axis=0, keepdims=True))
            pv = jnp.dot(vt_ref[0, pr * LANES:(pr + 1) * LANES, pl.ds(k0, tk)], p.astype(BF16),
                         preferred_element_type=F32)
            acc_ref[h] = alpha[h:h + 1, :] * acc_ref[h] + pv
        return m_new, alpha * l_all + jnp.concatenate(ls, axis=0)

    def attend(jp, carry):
        m_all, l_all, mx0 = carry
        mx1 = logits_pass(2 * jp + 1, 1)
        m_all, l_all = value_pass(2 * jp, 0, m_all, l_all, mx0)
        mx0 = logits_pass(jnp.minimum(2 * jp + 2, 2 * npair - 2), 0)
        m_all, l_all = value_pass(2 * jp + 1, 1, m_all, l_all, mx1)
        return m_all, l_all, mx0

    init = (jnp.full((ATT_HEADS, tq), NEG, F32), jnp.zeros((ATT_HEADS, tq), F32), logits_pass(0, 0))
    finish(lax.fori_loop(0, npair, attend, init)[1])


def _dsa_call(iqt, wit, qt, proj, tri, kidx, k, vt, b, l, tq, tk, online_max):
    nq = l // tq
    topk = min(TOPK_MAX, l // 4)
    kern = functools.partial(_dsa_kernel, tq=tq, tk=tk, topk=topk, online_max=online_max)
    s_stage = [pltpu.VMEM((2, ATT_HEADS, tk, tq), F32)]
    ntile = -(-(l // tk) // (4 * SUBLANES)) * (4 * SUBLANES)
    per_tile = [pltpu.VMEM((ntile, tq), F32)] * 3
    assert ntile <= 32 and (l // tk) % 2 == 0, "crossing tiles are flagged in one 32-bit word; tiles go in pairs"
    once = pl.Buffered(1)
    return pl.pallas_call(
        kern,
        grid=(b, nq),
        in_specs=[pl.BlockSpec((1, IDX_HEADS, IDX_K, tq), lambda bi, i: (bi, 0, 0, i)),
                  pl.BlockSpec((1, IDX_HEADS, tq), lambda bi, i: (bi, 0, i)),
                  pl.BlockSpec((1, ATT_WIDTH, tq), lambda bi, i: (bi, 0, i)),
                  pl.BlockSpec((tq, ATT_WIDTH), lambda bi, i: (bi * nq + i, 0)),
                  pl.BlockSpec((tk, tk), lambda bi, i: (0, 0), pipeline_mode=once),
                  pl.BlockSpec((1, l, IDX_K), lambda bi, i: (bi, 0, 0), pipeline_mode=once),
                  pl.BlockSpec((1, l, ATT_WIDTH), lambda bi, i: (bi, 0, 0), pipeline_mode=once),
                  pl.BlockSpec((1, ATT_WIDTH, l), lambda bi, i: (bi, 0, 0), pipeline_mode=once)],
        out_specs=pl.BlockSpec((1, tq, ATT_WIDTH), lambda bi, i: (bi, i, 0)),
        out_shape=jax.ShapeDtypeStruct((b, l, ATT_WIDTH), BF16),
        scratch_shapes=[pltpu.VMEM((l, tq), F32),
                        pltpu.VMEM((l, tq), BF16),
                        pltpu.VMEM((ATT_HEADS, LANES, tq), BF16),
                        pltpu.VMEM((ATT_HEADS, LANES, tq), F32)] + s_stage + per_tile + [
                            pltpu.VMEM((2, ATT_HEADS, tk, tq), BF16),
                            pltpu.VMEM((2, ATT_HEADS, tq), F32)],
        compiler_params=_cparams(("arbitrary", "arbitrary")),
        name="dsa_online_max" if online_max else "dsa",
    )(iqt, wit, qt, proj, tri, kidx, k, vt)


def _mm(a, b):
    return jnp.dot(a.astype(BF16), b.astype(BF16), preferred_element_type=F32)


def _mm_exact_lhs(a01, b):
    hi = b.astype(BF16)
    lo = (b - hi.astype(F32)).astype(BF16)
    a = a01.astype(BF16)
    return jnp.dot(a, hi, preferred_element_type=F32) + jnp.dot(a, lo, preferred_element_type=F32)


def _dn_kernel(dq_ref, dk_ref, dv_ref, dz_ref, dba_ref, cw_ref, avec_ref, bvec_ref, nw_ref,
               tril_ref, negl_ref, noteye_ref, o_ref, ext_ref, state_ref, *, rb):
    step = pl.program_id(1)
    halo = SUBLANES

    @pl.when(step == 0)
    def _():
        ext_ref[:, 0:halo, :] = jnp.zeros((3, halo, DN_WIDTH), F32)
        state_ref[...] = jnp.zeros(state_ref.shape, F32)

    def conv_silu(idx, src_ref):
        ext_ref[idx, halo:halo + rb, :] = src_ref[...]
        xe = ext_ref[idx]
        y = jnp.zeros((rb, DN_WIDTH), F32)
        for j in range(CONV_KERNEL):
            back = CONV_KERNEL - 1 - j
            xs = xe if back == 0 else pltpu.roll(xe, back, 0)
            y = y + xs[halo:halo + rb] * cw_ref[j:j + 1, idx * DN_WIDTH:(idx + 1) * DN_WIDTH]
        ext_ref[idx, 0:halo, :] = ext_ref[idx, rb:rb + halo, :]
        return y * jax.nn.sigmoid(y)

    def l2n(t):
        return t * lax.rsqrt(jnp.sum(t * t, axis=-1, keepdims=True) + EPS)

    qa = conv_silu(0, dq_ref)
    ka = conv_silu(1, dk_ref)
    va = conv_silu(2, dv_ref)

    dba = dba_ref[...]
    beta_all = jax.nn.sigmoid(dba)
    xg = dba + bvec_ref[...]
    softplus = jnp.maximum(xg, 0.0) + jnp.log1p(jnp.exp(-jnp.abs(xg)))
    g_all = -jnp.exp(avec_ref[...]) * softplus

    gc_all = _mm_exact_lhs(tril_ref[...], g_all)
    gc_rows = gc_all.T
    negl = negl_ref[...]
    noteye = noteye_ref[...]

    heads = range(DN_HEADS)
    sls = [slice(h * DN_HEAD_DIM, (h + 1) * DN_HEAD_DIM) for h in heads]
    qs = [l2n(qa[:, sls[h]]) * (DN_HEAD_DIM ** -0.5) for h in heads]
    ks = [l2n(ka[:, sls[h]]) for h in heads]
    gcs = [gc_all[:, DN_HEADS + h:DN_HEADS + h + 1] for h in heads]
    decays = [jnp.exp(gcs[h] - gc_rows[DN_HEADS + h:DN_HEADS + h + 1, :] + negl)
              for h in heads]
    k_betas = [ks[h] * beta_all[:, h:h + 1] for h in heads]
    kts = [ks[h].T for h in heads]
    nmats = [(_mm(k_betas[h], kts[h]) * decays[h] * noteye).astype(BF16) for h in heads]
    sols = [jnp.concatenate([va[:, sls[h]] * beta_all[:, h:h + 1], k_betas[h] * jnp.exp(gcs[h])], axis=-1)
            for h in heads]
    for it in range(6):
        sols = [sols[h] + jnp.dot(nmats[h], sols[h].astype(BF16), preferred_element_type=F32) for h in heads]
        if it < 5:
            nmats = [jnp.dot(nmats[h], nmats[h], preferred_element_type=F32).astype(BF16) for h in heads]
    folds = []
    for h in heads:
        intra = _mm(qs[h], kts[h]) * decays[h]
        fold = intra[:, 0:LANES]
        for t in range(1, rb // LANES):
            fold = fold + intra[:, t * LANES:(t + 1) * LANES]
        folds.append((fold + pltpu.roll(fold, CHUNK, 1))[:, 0:CHUNK])
    qgs = [qs[h] * jnp.exp(gcs[h]) for h in heads]
    states = [state_ref[h] for h in heads]
    outs = [[] for _ in heads]
    for ci in range(rb // CHUNK):
        cs = slice(ci * CHUNK, (ci + 1) * CHUNK)
        last = slice((ci + 1) * CHUNK - 1, (ci + 1) * CHUNK)
        on_state = [_mm(jnp.concatenate([sols[h][cs, DN_HEAD_DIM:], qgs[h][cs]], axis=0), states[h]) for h in heads]
        v_news = [sols[h][cs, 0:DN_HEAD_DIM] - on_state[h][0:CHUNK] for h in heads]
        kdecs = [ks[h][cs] * jnp.exp(gcs[h][last] - gcs[h][cs]) for h in heads]
        on_v = [_mm(jnp.concatenate([folds[h][cs], kdecs[h].T], axis=0), v_news[h]) for h in heads]
        for h in heads:
            outs[h].append(on_state[h][CHUNK:] + on_v[h][0:CHUNK])
        states = [states[h] * jnp.exp(gcs[h][last]) + on_v[h][CHUNK:] for h in heads]
    for h in heads:
        state_ref[h] = states[h]
        o = jnp.concatenate(outs[h], axis=0)
        z = dz_ref[:, sls[h]]
        on = o * lax.rsqrt(jnp.mean(o * o, axis=-1, keepdims=True) + EPS) * nw_ref[...]
        o_ref[:, sls[h]] = (on * (z * jax.nn.sigmoid(z))).astype(o_ref.dtype)


def _dn_call(proj, conv_w, avec, bvec, norm_w, b, l, rb=256):
    nr = l // rb
    kern = functools.partial(_dn_kernel, rb=rb)

    def col(base):
        return lambda bi, i: (bi * nr + i, (base - COL_DQ) // DN_WIDTH)

    r = jnp.arange(rb)
    in_lower = (r[:, None] // CHUNK == r[None, :] // CHUNK) & (r[:, None] >= r[None, :])
    tril = in_lower.astype(BF16)
    negl = jnp.where(in_lower, 0.0, NEG).astype(F32)
    noteye = -(r[:, None] != r[None, :]).astype(F32)
    const = pl.BlockSpec((rb, rb), lambda bi, i: (0, 0))

    return pl.pallas_call(
        kern,
        grid=(b, nr),
        in_specs=[pl.BlockSpec((rb, DN_WIDTH), col(COL_DQ)),
                  pl.BlockSpec((rb, DN_WIDTH), col(COL_DK)),
                  pl.BlockSpec((rb, DN_WIDTH), col(COL_DV)),
                  pl.BlockSpec((rb, DN_WIDTH), col(COL_DZ)),
                  pl.BlockSpec((rb, LANES), lambda bi, i: (bi * nr + i, (COL_DBA - COL_DQ) // LANES)),
                  pl.BlockSpec((CONV_KERNEL, 3 * DN_WIDTH), lambda bi, i: (0, 0)),
                  pl.BlockSpec((1, LANES), lambda bi, i: (0, 0)),
                  pl.BlockSpec((1, LANES), lambda bi, i: (0, 0)),
                  pl.BlockSpec((1, DN_HEAD_DIM), lambda bi, i: (0, 0)),
                  const, const, const],
        out_specs=pl.BlockSpec((rb, DN_WIDTH), lambda bi, i: (bi * nr + i, 0)),
        out_shape=jax.ShapeDtypeStruct((b * l, DN_WIDTH), BF16),
        scratch_shapes=[pltpu.VMEM((3, rb + SUBLANES, DN_WIDTH), F32),
                        pltpu.VMEM((DN_HEADS, DN_HEAD_DIM, DN_HEAD_DIM), F32)],
        compiler_params=_cparams(("arbitrary", "arbitrary")),
        name="deltanet",
    )(proj, proj, proj, proj, proj, conv_w, avec, bvec, norm_w, tril, negl, noteye)


def _out_kernel(x_ref, oa_ref, od_ref, wa_ref, wd_ref, o_ref):
    acc = jnp.dot(oa_ref[...], wa_ref[...], preferred_element_type=F32)
    acc = acc + jnp.dot(od_ref[...], wd_ref[...], preferred_element_type=F32)
    o_ref[...] = x_ref[...] + acc


def _out_call(x2, oa, od, wa, wd, tm=512):
    n = x2.shape[0]
    return pl.pallas_call(
        _out_kernel,
        grid=(n // tm,),
        in_specs=[pl.BlockSpec((tm, D_MODEL), lambda i: (i, 0)),
                  pl.BlockSpec((tm, ATT_WIDTH), lambda i: (i, 0)),
                  pl.BlockSpec((tm, DN_WIDTH), lambda i: (i, 0)),
                  pl.BlockSpec((ATT_WIDTH, D_MODEL), lambda i: (0, 0)),
                  pl.BlockSpec((DN_WIDTH, D_MODEL), lambda i: (0, 0))],
        out_specs=pl.BlockSpec((tm, D_MODEL), lambda i: (i, 0)),
        out_shape=jax.ShapeDtypeStruct((n, D_MODEL), F32),
        compiler_params=_cparams(("arbitrary",)),
        name="out_proj",
    )(x2, oa, od, wa, wd)


def _layer(h, ln_w, w_in, attn_q_norm_w, attn_k_norm_w, idx_k_norm_w, idx_k_norm_b,
           dn_conv_w, dn_a_log, dn_dt_bias, dn_norm_w, w_out):
    b, l, _ = h.shape
    x2 = h.reshape(b * l, D_MODEL)

    n_ikw = IDX_HEAD_DIM + IDX_HEADS
    src_ikw = COL_IQ + IDX_HEADS * IDX_HEAD_DIM
    src_dn = src_ikw + n_ikw
    src_dba = src_dn + 4 * DN_WIDTH
    w_pad = jnp.concatenate(
        [w_in[:, :src_ikw], w_in[:, src_dn:src_dba],
         w_in[:, src_ikw:src_dn], jnp.zeros((D_MODEL, LANES - n_ikw), F32),
         w_in[:, src_dba:], jnp.zeros((D_MODEL, LANES - 2 * DN_HEADS), F32)], axis=1).astype(BF16)
    grp = jnp.arange(ATT_WIDTH) // ATT_HEAD_DIM
    gmat = (grp[:, None] == grp[None, :]).astype(BF16)
    wq_t = jnp.tile(attn_q_norm_w, ATT_HEADS)[None, :]
    wk_t = jnp.tile(attn_k_norm_w, ATT_HEADS)[None, :]
    lnw_p = jnp.pad(idx_k_norm_w, (0, LANES - IDX_HEAD_DIM))[None, :]
    lnb_p = jnp.pad(idx_k_norm_b, (0, LANES - IDX_HEAD_DIM))[None, :]
    avec = jnp.pad(dn_a_log, (DN_HEADS, LANES - 2 * DN_HEADS))[None, :]
    bvec = jnp.pad(dn_dt_bias, (DN_HEADS, LANES - 2 * DN_HEADS))[None, :]
    tq = min(256, l)
    tk = min(256, l)
    tri = (jnp.arange(tk)[None, :] < jnp.arange(tk)[:, None]).astype(BF16)

    qt, k, vt, gate, iqt, kidx, wit, dn_in = _proj_call(x2, ln_w[None, :], w_pad, gmat, wq_t, wk_t,
                                                        lnw_p, lnb_p, b, l)
    logit_bound = (ATT_HEAD_DIM ** 0.5 * LOG2E) * jnp.max(jnp.abs(attn_q_norm_w)) * jnp.max(jnp.abs(attn_k_norm_w))
    dsa_args = (iqt, wit, qt, gate, tri, kidx, k, vt)
    o_a = lax.cond(logit_bound * BF16_SLACK < LOGIT_SAFE,
                   lambda *a: _dsa_call(*a, b, l, tq, tk, online_max=False),
                   lambda *a: _dsa_call(*a, b, l, tq, tk, online_max=True), *dsa_args)
    o_d = _dn_call(dn_in, dn_conv_w, avec, bvec, dn_norm_w[None, :], b, l)
    out = _out_call(x2, o_a.reshape(b * l, ATT_WIDTH), o_d,
                    w_out[:ATT_WIDTH].astype(BF16), w_out[ATT_WIDTH:].astype(BF16))
    return out.reshape(b, l, D_MODEL)


def kernel(x, ln_w, w_in, attn_q_norm_w, attn_k_norm_w, idx_k_norm_w, idx_k_norm_b, dn_conv_w, dn_A_log,
           dn_dt_bias, dn_norm_w, w_out):
    h = x
    for layer in range(ln_w.shape[0]):
        h = _layer(h, ln_w[layer], w_in[layer], attn_q_norm_w[layer], attn_k_norm_w[layer],
                   idx_k_norm_w[layer], idx_k_norm_b[layer], dn_conv_w[layer], dn_A_log[layer],
                   dn_dt_bias[layer], dn_norm_w[layer], w_out[layer])
    return h
```
